```python
import math
import jax, jax.numpy as jnp
from jax import lax
import numpy as np

D_MODEL = 2048
BATCH = 4
SEQ = 2048
DEPTH = 2

MEM_LEN = 256
NORM_EPS = 1e-6

RET_HEADS = 8
RET_DK = 128
RET_DV = 128
RET_CHUNK = 128
GDN_HEADS = 8
GDN_DK = 128
GDN_DV = 128
GDN_CONV = 4
GDN_CHUNK = 64
DIFF_HEADS = 4
DIFF_DK = 128
DIFF_DV = 256
Q_BLOCK = 128
MEM_HEADS = 4
MEM_DH = 128
D_FF = 5632
N_EXPERTS = 8
TOP_K = 2
EXPERT_FF = 7168

RET_QK = RET_HEADS * RET_DK
RET_V = RET_HEADS * RET_DV
GDN_QK = GDN_HEADS * GDN_DK
GDN_V = GDN_HEADS * GDN_DV
GDN_QKV = 2 * GDN_QK + GDN_V
DIFF_QK = DIFF_HEADS * 2 * DIFF_DK
DIFF_V = DIFF_HEADS * DIFF_DV
BRANCH_W = RET_V + GDN_V + DIFF_V
IN_SIZES = (RET_QK, RET_QK, RET_V, RET_V,
            GDN_QKV, GDN_V, GDN_HEADS, GDN_HEADS,
            DIFF_QK, DIFF_QK, DIFF_V,
            3 * D_MODEL)
IN_WIDTH = sum(IN_SIZES)
N_DENSE = (DEPTH + 1) // 2
N_MOE = DEPTH // 2

kernel_name = "hybrid_retention_gdn_diffattn_moe"


def rms_norm(x, w, eps=NORM_EPS):
    xf = x.astype(jnp.float32)
    y = xf * lax.rsqrt(jnp.mean(xf * xf, axis=-1, keepdims=True) + eps)
    return (y * w.astype(jnp.float32)).astype(x.dtype)


def head_layer_norm(o, eps=NORM_EPS):
    mu = jnp.mean(o, axis=-1, keepdims=True)
    var = jnp.mean(jnp.square(o - mu), axis=-1, keepdims=True)
    return (o - mu) * lax.rsqrt(var + eps)


def l2_normalize(t, eps=1e-6):
    return t * lax.rsqrt(jnp.sum(t * t, axis=-1, keepdims=True) + eps)


def split_heads(t, h):
    b, s, _ = t.shape
    return t.reshape(b, s, h, -1).transpose(0, 2, 1, 3)


def merge_heads(t):
    b, h, s, d = t.shape
    return t.transpose(0, 2, 1, 3).reshape(b, s, h * d)


def split_points(sizes):
    pts, acc = [], 0
    for sz in sizes[:-1]:
        acc += sz
        pts.append(acc)
    return pts


def retention(q, k, v):
    b, h, s, dk = q.shape
    dv = v.shape[-1]
    c = RET_CHUNK
    n = s // c
    log_g = jnp.log(1.0 - 2.0 ** (-5.0 - jnp.arange(h, dtype=jnp.float32)))
    k = k * dk ** -0.5
    qc = q.reshape(b, h, n, c, dk)
    kc = k.reshape(b, h, n, c, dk)
    vc = v.reshape(b, h, n, c, dv)
    pos = jnp.arange(c, dtype=jnp.float32)
    diff = pos[:, None] - pos[None, :]
    causal = diff >= 0
    decay_mask = jnp.where(causal, jnp.exp(log_g[:, None, None] * jnp.where(causal, diff, 0.0)), 0.0)
    inner = jnp.einsum('bhnik,bhnjk->bhnij', qc, kc) * decay_mask[None, :, None]
    o_inner = jnp.einsum('bhnij,bhnjv->bhniv', inner, vc)
    to_end = jnp.exp(log_g[:, None] * (c - 1.0 - pos)[None, :])
    kv = jnp.einsum('bhnjk,bhnjv->nbhkv', kc * to_end[None, :, None, :, None], vc)
    chunk_decay = jnp.exp(log_g * c)[None, :, None, None]

    def step(state, kv_n):
        return chunk_decay * state + kv_n, state

    _, prev = lax.scan(step, jnp.zeros((b, h, dk, dv), jnp.float32), kv)
    from_start = jnp.exp(log_g[:, None] * (pos + 1.0)[None, :])
    o_cross = jnp.einsum('bhnik,nbhkv->bhniv', qc, prev) * from_start[None, :, None, :, None]
    return (o_inner + o_cross).reshape(b, h, s, dv)


def causal_depthwise_conv(x, w):
    kw, ch = w.shape
    return lax.conv_general_dilated(x, w[:, None, :], window_strides=(1,), padding=[(kw - 1, 0)],
                                    dimension_numbers=('NWC', 'WIO', 'NWC'), feature_group_count=ch)


def gated_delta_rule(q, k, v, g, beta):
    b, h, s, dk = q.shape
    dv = v.shape[-1]
    c = GDN_CHUNK
    n = s // c
    q = q * dk ** -0.5
    qc = q.reshape(b, h, n, c, dk)
    kc = k.reshape(b, h, n, c, dk)
    vc = v.reshape(b, h, n, c, dv)
    bc = beta.reshape(b, h, n, c)
    G = jnp.cumsum(g.reshape(b, h, n, c), axis=-1)
    idx = jnp.arange(c)
    lower_incl = idx[:, None] >= idx[None, :]
    strict = idx[:, None] > idx[None, :]
    seg = G[..., :, None] - G[..., None, :]
    L = jnp.where(lower_incl, jnp.exp(jnp.where(lower_incl, seg, 0.0)), 0.0)
    A = jnp.where(strict, jnp.einsum('bhnik,bhnjk->bhnij', kc, kc) * L * bc[..., :, None], 0.0)
    M = jnp.eye(c, dtype=jnp.float32) + A
    rhs = jnp.concatenate([kc * (bc * jnp.exp(G))[..., None], vc * bc[..., None]], axis=-1)
    sol = lax.linalg.triangular_solve(M, rhs, left_side=True, lower=True, unit_diagonal=True)
    w_c, u_c = sol[..., :dk], sol[..., dk:]
    q_dec = qc * jnp.exp(G)[..., None]
    a_qk = jnp.einsum('bhnik,bhnjk->bhnij', qc, kc) * L
    g_last = G[..., -1]
    k_tail = kc * jnp.exp(g_last[..., None] - G)[..., None]
    xs = (jnp.moveaxis(q_dec, 2, 0), jnp.moveaxis(a_qk, 2, 0), jnp.moveaxis(w_c, 2, 0),
          jnp.moveaxis(u_c, 2, 0), jnp.moveaxis(k_tail, 2, 0), jnp.moveaxis(g_last, 2, 0))

    def step(S, inp):
        dq, aqk, wn, un, kt, gl = inp
        v_new = un - jnp.einsum('bhck,bhkv->bhcv', wn, S)
        o = jnp.einsum('bhck,bhkv->bhcv', dq, S) + jnp.einsum('bhij,bhjv->bhiv', aqk, v_new)
        S = S * jnp.exp(gl)[..., None, None] + jnp.einsum('bhck,bhcv->bhkv', kt, v_new)
        return S, o

    _, o = lax.scan(step, jnp.zeros((b, h, dk, dv), jnp.float32), xs)
    return jnp.moveaxis(o, 0, 2).reshape(b, h, s, dv)


def alibi_slopes(n):
    return 2.0 ** (-8.0 * (jnp.arange(n, dtype=jnp.float32) + 1.0) / n)


def diff_attention(q, k, v, lam):
    b, h, _, s, dk = q.shape
    scale = dk ** -0.5
    slopes = alibi_slopes(h)[:, None, None, None]
    outs = []
    for i in range(s // Q_BLOCK):
        lo, hi = i * Q_BLOCK, (i + 1) * Q_BLOCK
        dist = (jnp.arange(lo, hi)[:, None] - jnp.arange(hi)[None, :]).astype(jnp.float32)
        sc = jnp.einsum('bhmqd,bhmkd->bhmqk', q[:, :, :, lo:hi], k[:, :, :, :hi]) * scale - slopes * dist
        sc = jnp.where(dist >= 0, sc, -jnp.inf)
        p = jax.nn.softmax(sc, axis=-1)
        a = p[:, :, 0] - lam * p[:, :, 1]
        outs.append(jnp.einsum('bhqk,bhkv->bhqv', a, v[:, :, :hi]))
    return jnp.concatenate(outs, axis=2)


def hybrid_mixer(xn, w_in, conv_w, a_log, dt_bias, gdn_norm_w, lam_vecs, diff_norm_w, w_branch, w_out, lambda_init):
    b, s, _ = xn.shape
    proj = (xn @ w_in).astype(jnp.float32)
    (rq, rk, rv, rg, bqkv, bz, bb, ba, cq, ck, cv, gates) = jnp.split(proj, split_points(IN_SIZES), axis=-1)

    o_a = retention(split_heads(rq, RET_HEADS), split_heads(rk, RET_HEADS), split_heads(rv, RET_HEADS))
    o_a = merge_heads(head_layer_norm(o_a)) * jax.nn.silu(rg)

    qkv = jax.nn.silu(causal_depthwise_conv(bqkv, conv_w.astype(jnp.float32)))
    bq, bk, bv = jnp.split(qkv, [GDN_QK, 2 * GDN_QK], axis=-1)
    beta = jax.nn.sigmoid(bb).transpose(0, 2, 1)
    g = (-jnp.exp(a_log.astype(jnp.float32)) * jax.nn.softplus(ba + dt_bias.astype(jnp.float32))).transpose(0, 2, 1)
    o_b = gated_delta_rule(l2_normalize(split_heads(bq, GDN_HEADS)), l2_normalize(split_heads(bk, GDN_HEADS)),
                           split_heads(bv, GDN_HEADS), g, beta)
    o_b = merge_heads(rms_norm(o_b, gdn_norm_w)) * jax.nn.silu(bz)

    q2 = cq.reshape(b, s, DIFF_HEADS, 2, DIFF_DK).transpose(0, 2, 3, 1, 4)
    k2 = ck.reshape(b, s, DIFF_HEADS, 2, DIFF_DK).transpose(0, 2, 3, 1, 4)
    lv = lam_vecs.astype(jnp.float32)
    lam = jnp.exp(jnp.sum(lv[0] * lv[1])) - jnp.exp(jnp.sum(lv[2] * lv[3])) + lambda_init
    o_c = diff_attention(q2, k2, split_heads(cv, DIFF_HEADS), lam)
    o_c = merge_heads(rms_norm(o_c, diff_norm_w) * (1.0 - lambda_init))

    ga, gb, gc = jnp.split(jax.nn.sigmoid(gates), 3, axis=-1)
    wa, wb, wc = jnp.split(w_branch, [RET_V, RET_V + GDN_V], axis=0)
    merged = ga * (o_a @ wa) + gb * (o_b @ wb) + gc * (o_c @ wc)
    return (merged @ w_out).astype(xn.dtype)


def memory_cross_attention(xn, mem_n, w_q, w_kv, w_o):
    q = split_heads(xn @ w_q, MEM_HEADS)
    k, v = jnp.split(mem_n @ w_kv, 2, axis=-1)
    k, v = split_heads(k, MEM_HEADS), split_heads(v, MEM_HEADS)
    sc = jnp.einsum('bhqd,bhkd->bhqk', q, k).astype(jnp.float32) * MEM_DH ** -0.5
    p = jax.nn.softmax(sc, axis=-1)
    o = jnp.einsum('bhqk,bhkd->bhqd', p, v.astype(jnp.float32))
    return (merge_heads(o) @ w_o).astype(xn.dtype)


def swiglu(x, w_gu, w_down):
    a, u = jnp.split(x @ w_gu, 2, axis=-1)
    return (jax.nn.silu(a) * u) @ w_down


def moe_swiglu(x, w_router, w_gu, w_down):
    b, s, d = x.shape
    xf = x.reshape(b * s, d)
    logits = (xf @ w_router).astype(jnp.float32)
    top_val, top_idx = lax.top_k(logits, TOP_K)
    top_w = jax.nn.softmax(top_val, axis=-1)
    combine = jnp.sum(jax.nn.one_hot(top_idx, N_EXPERTS, dtype=jnp.float32) * top_w[..., None], axis=1)
    out = jnp.zeros((b * s, d), jnp.float32)
    for e in range(N_EXPERTS):
        out = out + combine[:, e:e + 1] * swiglu(xf, w_gu[e], w_down[e])
    return out.reshape(b, s, d).astype(x.dtype)


def _normal(key, shape, fan_in):
    return jax.random.normal(key, shape, jnp.float32) * fan_in ** -0.5


def _gain(key, shape):
    return 1.0 + 0.02 * jax.random.normal(key, shape, jnp.float32)


def setup_inputs(seed: int = 0) -> dict:
    key = jax.random.key(seed)
    ks = jax.random.split(key, 26)
    dt = jnp.exp(jax.random.uniform(ks[5], (DEPTH, GDN_HEADS), jnp.float32, math.log(1e-3), math.log(1e-1)))
    return {
        "x": jax.random.normal(ks[0], (BATCH, SEQ, D_MODEL), jnp.float32),
        "mem": jax.random.normal(ks[1], (BATCH, MEM_LEN, D_MODEL), jnp.float32),
        "w_in": _normal(ks[2], (DEPTH, D_MODEL, IN_WIDTH), D_MODEL),
        "conv_w": _normal(ks[3], (DEPTH, GDN_CONV, GDN_QKV), GDN_CONV),
        "gdn_a_log": jnp.log(jax.random.uniform(ks[4], (DEPTH, GDN_HEADS), jnp.float32, 1.0, 16.0)),
        "gdn_dt_bias": dt + jnp.log(-jnp.expm1(-dt)),
        "gdn_norm_w": _gain(ks[6], (DEPTH, GDN_DV)),
        "diff_lambda": 0.1 * jax.random.normal(ks[7], (DEPTH, 4, DIFF_DK), jnp.float32),
        "diff_norm_w": _gain(ks[8], (DEPTH, DIFF_DV)),
        "w_branch": _normal(ks[9], (DEPTH, BRANCH_W, D_MODEL), RET_V),
        "w_out": _normal(ks[10], (DEPTH, D_MODEL, D_MODEL), D_MODEL),
        "mix_norm_w": _gain(ks[11], (DEPTH, D_MODEL)),
        "mem_norm_w": _gain(ks[12], (DEPTH, D_MODEL)),
        "cross_norm_w": _gain(ks[13], (DEPTH, D_MODEL)),
        "cross_w_q": _normal(ks[14], (DEPTH, D_MODEL, MEM_HEADS * MEM_DH), D_MODEL),
        "cross_w_kv": _normal(ks[15], (DEPTH, D_MODEL, 2 * MEM_HEADS * MEM_DH), D_MODEL),
        "cross_w_o": _normal(ks[16], (DEPTH, MEM_HEADS * MEM_DH, D_MODEL), MEM_HEADS * MEM_DH),
        "ffn_norm_w": _gain(ks[17], (DEPTH, D_MODEL)),
        "dense_w_gu": _normal(ks[18], (N_DENSE, D_MODEL, 2 * D_FF), D_MODEL),
        "dense_w_down": _normal(ks[19], (N_DENSE, D_FF, D_MODEL), D_FF),
        "router_w": _normal(ks[20], (N_MOE, D_MODEL, N_EXPERTS), D_MODEL),
        "expert_w_gu": _normal(ks[21], (N_MOE, N_EXPERTS, D_MODEL, 2 * EXPERT_FF), D_MODEL),
        "expert_w_down": _normal(ks[22], (N_MOE, N_EXPERTS, EXPERT_FF, D_MODEL), EXPERT_FF),
        "final_norm_w": _gain(ks[23], (D_MODEL,)),
    }


def reference(x, mem, w_in, conv_w, gdn_a_log, gdn_dt_bias, gdn_norm_w, diff_lambda, diff_norm_w,
              w_branch, w_out, mix_norm_w, mem_norm_w, cross_norm_w, cross_w_q, cross_w_kv, cross_w_o,
              ffn_norm_w, dense_w_gu, dense_w_down, router_w, expert_w_gu, expert_w_down, final_norm_w):
    for l in range(DEPTH):
        lambda_init = 0.8 - 0.6 * math.exp(-0.3 * l)
        h = rms_norm(x, mix_norm_w[l])
        x = x + hybrid_mixer(h, w_in[l], conv_w[l], gdn_a_log[l], gdn_dt_bias[l], gdn_norm_w[l],
                             diff_lambda[l], diff_norm_w[l], w_branch[l], w_out[l], lambda_init).astype(x.dtype)
        h = rms_norm(x, cross_norm_w[l])
        m = rms_norm(mem, mem_norm_w[l])
        x = x + memory_cross_attention(h, m, cross_w_q[l], cross_w_kv[l], cross_w_o[l]).astype(x.dtype)
        h = rms_norm(x, ffn_norm_w[l])
        if l % 2 == 0:
            x = x + swiglu(h, dense_w_gu[l // 2], dense_w_down[l // 2]).astype(x.dtype)
        else:
            x = x + moe_swiglu(h, router_w[l // 2], expert_w_gu[l // 2], expert_w_down[l // 2]).astype(x.dtype)
    return rms_norm(x, final_norm_w)
```

```python
import functools
import math

import jax
import jax.numpy as jnp
from jax import lax
from jax.experimental import pallas as pl
from jax.experimental.pallas import tpu as pltpu

F32 = jnp.float32
BF16 = jnp.bfloat16

D_MODEL = 2048
DEPTH = 2
NORM_EPS = 1e-6
RET_HEADS, RET_DK, RET_CHUNK = 8, 128, 128
GDN_HEADS, GDN_DK, GDN_CONV, GDN_CHUNK = 8, 128, 4, 64
DIFF_HEADS, DIFF_DK, DIFF_DV = 4, 128, 256
MEM_HEADS, MEM_DH = 4, 128
D_FF = 5632
N_EXPERTS, TOP_K, EXPERT_FF = 8, 2, 7168

RET_W = RET_HEADS * RET_DK
GDN_W = GDN_HEADS * GDN_DK
DIFF_W = DIFF_HEADS * DIFF_DV
OFF_GDN = 4 * RET_W
OFF_SMALL = OFF_GDN + 4 * GDN_W
OFF_DIFF = OFF_SMALL + 2 * GDN_HEADS
TAIL_W = 3 * DIFF_W + 3 * D_MODEL

V7X_LANES = 128
V7X_MXU_DIM = 256
V7X_VMEM_BYTES = 64 * 1024 * 1024
VMEM_LIMIT = (V7X_VMEM_BYTES * 3) // 4

NEG_BIG = -1e30


def _params(*semantics):
    return pltpu.CompilerParams(dimension_semantics=semantics, vmem_limit_bytes=VMEM_LIMIT)


def _sigmoid(x):
    return 1.0 / (1.0 + jnp.exp(-x))


def _silu(x):
    return x * _sigmoid(x)


def _dot(a, b):
    return jnp.dot(a, b, preferred_element_type=F32)


def _dot_nt(a, b):
    return lax.dot_general(a, b, (((1,), (1,)), ((), ())), preferred_element_type=F32)


def _dot_tn(a, b):
    return lax.dot_general(a, b, (((0,), (0,)), ((), ())), preferred_element_type=F32)


def _split_bf16(x):
    hi = x.astype(BF16)
    lo = (x - hi.astype(F32)).astype(BF16)
    return hi, lo


def _dot_f32ish(a, b):
    ah, al = _split_bf16(a)
    bh, bl = _split_bf16(b)
    lhs = jnp.concatenate([ah, ah, al], axis=1)
    rhs = jnp.concatenate([bh, bl, bh], axis=0)
    return _dot(lhs, rhs)


def _rmsnorm_kernel(x_ref, w_ref, o_ref):
    x = x_ref[...]
    ms = jnp.mean(x * x, axis=-1, keepdims=True)
    o_ref[...] = (x * lax.rsqrt(ms + NORM_EPS) * w_ref[...]).astype(o_ref.dtype)


def rmsnorm(x, w, out_dtype, bm=512):
    m, d = x.shape
    bm = min(bm, m)
    return pl.pallas_call(
        _rmsnorm_kernel,
        grid=(m // bm,),
        in_specs=[pl.BlockSpec((bm, d), lambda i: (i, 0)), pl.BlockSpec((1, d), lambda i: (0, 0))],
        out_specs=pl.BlockSpec((bm, d), lambda i: (i, 0)),
        out_shape=jax.ShapeDtypeStruct((m, d), out_dtype),
        compiler_params=_params("parallel"),
        name="rmsnorm",
    )(x, w.reshape(1, d))


def _matmul_kernel(x_ref, w_ref, *rest, has_res):
    if has_res:
        res_ref, o_ref, wbf_ref = rest
    else:
        o_ref, wbf_ref = rest

    @pl.when(pl.program_id(1) == 0)
    def _convert_weight_tile():
        wbf_ref[...] = w_ref[...].astype(BF16)

    acc = _dot(x_ref[...].astype(BF16), wbf_ref[...])
    if has_res:
        acc = acc + res_ref[...]
    o_ref[...] = acc.astype(o_ref.dtype)


def matmul(x, w, w_lead, col_block0, n_out, *, bm, bn, out_dtype, res=None, name="matmul"):
    m, k = x.shape
    bm = min(bm, m)
    bn = min(bn, n_out)
    assert m % bm == 0 and n_out % bn == 0 and w.shape[-2] == k
    n_lead = len(w_lead)
    in_specs = [
        pl.BlockSpec((bm, k), lambda j, i: (i, 0)),
        pl.BlockSpec((None,) * n_lead + (k, bn), lambda j, i: (*w_lead, 0, col_block0 + j)),
    ]
    args = [x, w]
    if res is not None:
        in_specs.append(pl.BlockSpec((bm, bn), lambda j, i: (i, j)))
        args.append(res)
    return pl.pallas_call(
        functools.partial(_matmul_kernel, has_res=res is not None),
        grid=(n_out // bn, m // bm),
        in_specs=in_specs,
        out_specs=pl.BlockSpec((bm, bn), lambda j, i: (i, j)),
        out_shape=jax.ShapeDtypeStruct((m, n_out), out_dtype),
        scratch_shapes=[pltpu.VMEM((k, bn), BF16)],
        compiler_params=_params("arbitrary", "arbitrary"),
        name=name,
    )(*args)


def _retention_kernel(lg_ref, q_ref, k_ref, v_ref, g_ref, o_ref):
    s, dk = q_ref.shape
    c = RET_CHUNK
    lg = lg_ref[...]
    row = lax.broadcasted_iota(jnp.int32, (c, c), 0).astype(F32)
    col = lax.broadcasted_iota(jnp.int32, (c, c), 1).astype(F32)
    diff = row - col
    causal = diff >= 0
    scale = dk ** -0.5
    decay = jnp.where(causal, jnp.exp(lg * jnp.where(causal, diff, 0.0)), 0.0) * scale
    to_end = jnp.exp(lg * (c - 1.0 - row)) * scale
    from_start = jnp.exp(lg * (row + 1.0))
    chunk_decay = jnp.exp(lg * float(c))

    def body(i, state):
        r = pl.ds(pl.multiple_of(i * c, c), c)
        kf = k_ref[r, :]
        qc = q_ref[r, :].astype(BF16)
        vc = v_ref[r, :].astype(BF16)
        inner = _dot_nt(qc, kf.astype(BF16)) * decay
        o = _dot(inner.astype(BF16), vc)
        o = o + _dot(qc, state.astype(BF16)) * from_start
        state = chunk_decay * state + _dot_tn((kf * to_end).astype(BF16), vc)
        mu = jnp.mean(o, axis=-1, keepdims=True)
        var = jnp.mean(jnp.square(o - mu), axis=-1, keepdims=True)
        on = (o - mu) * lax.rsqrt(var + NORM_EPS)
        o_ref[r, :] = (on * _silu(g_ref[r, :])).astype(o_ref.dtype)
        return state

    lax.fori_loop(0, s // c, body, jnp.zeros((dk, dk), F32))


def retention_mixer(proj, batch, seq):
    assert RET_DK == RET_CHUNK == V7X_LANES
    h = RET_HEADS
    log_g = jnp.log(1.0 - 2.0 ** (-5.0 - jnp.arange(h, dtype=F32)))
    lg = jnp.broadcast_to(log_g[:, None, None], (h, 1, V7X_LANES))

    def col(base):
        return pl.BlockSpec((seq, RET_DK), lambda b, hh: (b, base * h + hh))

    return pl.pallas_call(
        _retention_kernel,
        grid=(batch, h),
        in_specs=[pl.BlockSpec((None, 1, V7X_LANES), lambda b, hh: (hh, 0, 0)), col(0), col(1), col(2), col(3)],
        out_specs=pl.BlockSpec((seq, RET_DK), lambda b, hh: (b, hh)),
        out_shape=jax.ShapeDtypeStruct((batch * seq, RET_W), BF16),
        compiler_params=_params("parallel", "parallel"),
        name="retention",
    )(lg, proj, proj, proj, proj)


GDN_GROUP = V7X_MXU_DIM


def _causal_conv_silu(x, w):
    row = lax.broadcasted_iota(jnp.int32, x.shape, 0)
    y = x * w[GDN_CONV - 1:GDN_CONV, :]
    for sh in range(1, GDN_CONV):
        xs = jnp.where(row >= sh, pltpu.roll(x, sh, 0), 0.0)
        y = y + xs * w[GDN_CONV - 1 - sh:GDN_CONV - sh, :]
    return _silu(y)


def _l2_normalize(t):
    return t * lax.rsqrt(jnp.sum(t * t, axis=-1, keepdims=True) + 1e-6)


def _gdn_kernel(xq_ref, xk_ref, xv_ref, z_ref, sm_ref, cwq_ref, cwk_ref, cwv_ref, alog_ref, dtb_ref, nw_ref,
                o_ref, q_s, k_s, v_s, g_s, b_s):
    s, dk = xq_ref.shape
    head = pl.program_id(1)
    gsz, c = GDN_GROUP, GDN_CHUNK
    n_sub = gsz // c

    q_s[...] = _l2_normalize(_causal_conv_silu(xq_ref[...], cwq_ref[...])) * (dk ** -0.5)
    k_s[...] = _l2_normalize(_causal_conv_silu(xk_ref[...], cwk_ref[...]))
    v_s[...] = _causal_conv_silu(xv_ref[...], cwv_ref[...])
    sm = sm_ref[...]
    lane = lax.broadcasted_iota(jnp.int32, sm.shape, 1)
    z_dec = sm + dtb_ref[...]
    softplus = jnp.maximum(z_dec, 0.0) + jnp.log(1.0 + jnp.exp(-jnp.abs(z_dec)))
    g_all = -jnp.exp(alog_ref[...]) * softplus
    g_col = jnp.sum(jnp.where(lane == head + GDN_HEADS, g_all, 0.0), axis=-1, keepdims=True)
    b_col = jnp.sum(jnp.where(lane == head, _sigmoid(sm), 0.0), axis=-1, keepdims=True)
    g_s[...] = jnp.broadcast_to(g_col, (s, dk))
    b_s[...] = jnp.broadcast_to(b_col, (s, dk))

    ri = lax.broadcasted_iota(jnp.int32, (gsz, gsz), 0)
    ci = lax.broadcasted_iota(jnp.int32, (gsz, gsz), 1)
    shift = int(math.log2(c))
    same = jnp.right_shift(ri, shift) == jnp.right_shift(ci, shift)
    incl = same & (ri >= ci)
    strict = same & (ri > ci)
    cum_mat = jnp.concatenate([jnp.where(incl, 1.0, 0.0), jnp.where(same, 1.0, 0.0)], axis=0).astype(BF16)
    eye = jnp.where(ri == ci, 1.0, 0.0)
    nw = nw_ref[...]

    def group(gi, state):
        r = pl.ds(pl.multiple_of(gi * gsz, gsz), gsz)
        q, k, v = q_s[r, :], k_s[r, :], v_s[r, :]
        gb, bb = g_s[r, :], b_s[r, :]
        g1 = gb.astype(BF16)
        r1 = gb - g1.astype(F32)
        g2 = r1.astype(BF16)
        g3 = (r1 - g2.astype(F32)).astype(BF16)
        cums = _dot(cum_mat, jnp.concatenate([g1, g2, g3], axis=1))
        cums = cums[:, :dk] + cums[:, dk:2 * dk] + cums[:, 2 * dk:]
        gc, g_last = cums[:gsz], cums[gsz:]
        gc2 = jnp.concatenate([gc, gc], axis=1)
        seg = gc2 - gc2.T
        decay_l = jnp.where(incl, jnp.exp(jnp.where(incl, seg, 0.0)), 0.0)
        kb = k.astype(BF16)
        kk = _dot_nt(kb, kb)
        bb2 = jnp.concatenate([bb, bb], axis=1)
        a = jnp.where(strict, kk * decay_l * bb2, 0.0)
        p = eye - a
        pw = a
        for _ in range(int(math.log2(c)) - 1):
            pw = _dot_f32ish(pw, pw)
            p = p + _dot_f32ish(p, pw)
        eg = jnp.exp(gc)
        rhs = jnp.concatenate([k * (bb * eg), v * bb], axis=1)
        wu = _dot(p.astype(BF16), rhs.astype(BF16))
        a_qk = (_dot_nt(q.astype(BF16), kb) * decay_l).astype(BF16)
        q_dec = (q * eg).astype(BF16)
        k_tail = (k * jnp.exp(g_last - gc)).astype(BF16)
        e_last = jnp.exp(g_last)
        outs = []
        for ci_ in range(n_sub):
            lo, hi = ci_ * c, (ci_ + 1) * c
            sb = state.astype(BF16)
            v_new = wu[lo:hi, dk:] - _dot(wu[lo:hi, :dk].astype(BF16), sb)
            vb = v_new.astype(BF16)
            pieces = []
            if lo:
                pieces.append(jnp.zeros((lo, dk), BF16))
            pieces.append(vb)
            if gsz - hi:
                pieces.append(jnp.zeros((gsz - hi, dk), BF16))
            v_pad = jnp.concatenate(pieces, axis=0) if len(pieces) > 1 else vb
            outs.append(_dot(q_dec[lo:hi], sb) + _dot(a_qk[lo:hi], v_pad))
            state = state * e_last[lo:lo + 1, :] + _dot_tn(k_tail[lo:hi], vb)
        o = jnp.concatenate(outs, axis=0)
        on = o * lax.rsqrt(jnp.mean(o * o, axis=-1, keepdims=True) + NORM_EPS) * nw
        o_ref[r, :] = (on * _silu(z_ref[r, :])).astype(o_ref.dtype)
        return state

    lax.fori_loop(0, s // gsz, group, jnp.zeros((dk, dk), F32))


def gdn_mixer(proj, small, conv_w, a_log, dt_bias, norm_w, layer, batch, seq):
    assert GDN_DK == V7X_LANES and GDN_GROUP % GDN_CHUNK == 0 and seq % GDN_GROUP == 0
    h = GDN_HEADS
    base = OFF_GDN // GDN_DK
    pad = V7X_LANES - 2 * h
    alog_row = jnp.concatenate([jnp.zeros((h,), F32), a_log[layer], jnp.zeros((pad,), F32)]).reshape(1, V7X_LANES)
    dtb_row = jnp.concatenate([jnp.zeros((h,), F32), dt_bias[layer], jnp.zeros((pad,), F32)]).reshape(1, V7X_LANES)

    def col(j):
        return pl.BlockSpec((seq, GDN_DK), lambda b, hh: (b, base + j * h + hh))

    def cw(j):
        return pl.BlockSpec((None, GDN_CONV, GDN_DK), lambda b, hh: (layer, 0, j * h + hh))

    row_spec = pl.BlockSpec((1, V7X_LANES), lambda b, hh: (0, 0))
    return pl.pallas_call(
        _gdn_kernel,
        grid=(batch, h),
        in_specs=[col(0), col(1), col(2), col(3),
                  pl.BlockSpec((seq, V7X_LANES), lambda b, hh: (b, 0)),
                  cw(0), cw(1), cw(2), row_spec, row_spec, row_spec],
        out_specs=pl.BlockSpec((seq, GDN_DK), lambda b, hh: (b, hh)),
        out_shape=jax.ShapeDtypeStruct((batch * seq, GDN_W), BF16),
        scratch_shapes=[pltpu.VMEM((seq, GDN_DK), F32)] * 5,
        compiler_params=_params("parallel", "parallel"),
        name="gated_deltanet",
    )(proj, proj, proj, proj, small, conv_w, conv_w, conv_w, alog_row, dtb_row, norm_w[layer].reshape(1, GDN_DK))


DIFF_TQ = 256
DIFF_TK = 256


def _diff_attn_kernel(slope_ref, lam_ref, nw_ref, q_ref, k_ref, v_ref, o_ref, *, out_scale, lambda_init):
    tq, tk, dk = DIFF_TQ, DIFF_TK, DIFF_DK
    qi = pl.program_id(2)
    slope = slope_ref[...][:, :1]
    lv = lam_ref[...]
    lam = (jnp.exp(jnp.sum(lv[0:1] * lv[1:2], axis=-1, keepdims=True))
           - jnp.exp(jnp.sum(lv[2:3] * lv[3:4], axis=-1, keepdims=True)) + lambda_init)
    q = q_ref[...] * (dk ** -0.5)
    q1, q2 = q[:, :dk].astype(BF16), q[:, dk:].astype(BF16)
    rel = (lax.broadcasted_iota(jnp.int32, (tq, tk), 0) - lax.broadcasted_iota(jnp.int32, (tq, tk), 1)).astype(F32)

    def body(j, carry):
        m1, l1, a1, m2, l2, a2 = carry
        r = pl.ds(pl.multiple_of(j * tk, tk), tk)
        kb = k_ref[r, :].astype(BF16)
        vb = v_ref[r, :].astype(BF16)
        dist = rel + ((qi - j) * tk).astype(F32)
        bias = slope * dist

        def update(qh, kh, m, l, acc):
            sc = jnp.where(dist >= 0, _dot_nt(qh, kh) - bias, NEG_BIG)
            m_new = jnp.maximum(m, jnp.max(sc, axis=-1, keepdims=True))
            alpha = jnp.exp(m - m_new)
            p = jnp.exp(sc - m_new)
            l = alpha * l + jnp.sum(p, axis=-1, keepdims=True)
            acc = alpha * acc + _dot(p.astype(BF16), vb)
            return m_new, l, acc

        m1, l1, a1 = update(q1, kb[:, :dk], m1, l1, a1)
        m2, l2, a2 = update(q2, kb[:, dk:], m2, l2, a2)
        return m1, l1, a1, m2, l2, a2

    neg = jnp.full((tq, 1), NEG_BIG, F32)
    zero = jnp.zeros((tq, 1), F32)
    zacc = jnp.zeros((tq, DIFF_DV), F32)
    m1, l1, a1, m2, l2, a2 = lax.fori_loop(0, qi + 1, body, (neg, zero, zacc, neg, zero, zacc))
    o = a1 / l1 - lam * (a2 / l2)
    on = o * lax.rsqrt(jnp.mean(o * o, axis=-1, keepdims=True) + NORM_EPS) * nw_ref[...]
    o_ref[...] = (on * out_scale).astype(o_ref.dtype)


def diff_attn_mixer(tail, lam_vecs, norm_w, layer, lambda_init, batch, seq):
    assert DIFF_TQ == DIFF_TK and seq % DIFF_TQ == 0 and 2 * DIFF_DK == DIFF_DV
    h = DIFF_HEADS
    slopes = 2.0 ** (-8.0 * (jnp.arange(h, dtype=F32) + 1.0) / h)
    slope_rows = jnp.broadcast_to(slopes[:, None, None], (h, 1, V7X_LANES))
    kernel = functools.partial(_diff_attn_kernel, out_scale=1.0 - lambda_init, lambda_init=lambda_init)
    return pl.pallas_call(
        kernel,
        grid=(batch, h, seq // DIFF_TQ),
        in_specs=[
            pl.BlockSpec((None, 1, V7X_LANES), lambda b, hh, i: (hh, 0, 0)),
            pl.BlockSpec((None, 4, DIFF_DK), lambda b, hh, i: (layer, 0, 0)),
            pl.BlockSpec((1, DIFF_DV), lambda b, hh, i: (0, 0)),
            pl.BlockSpec((DIFF_TQ, 2 * DIFF_DK), lambda b, hh, i: (b * (seq // DIFF_TQ) + i, hh)),
            pl.BlockSpec((seq, 2 * DIFF_DK), lambda b, hh, i: (b, h + hh)),
            pl.BlockSpec((seq, DIFF_DV), lambda b, hh, i: (b, 2 * h + hh)),
        ],
        out_specs=pl.BlockSpec((DIFF_TQ, DIFF_DV), lambda b, hh, i: (b * (seq // DIFF_TQ) + i, hh)),
        out_shape=jax.ShapeDtypeStruct((batch * seq, DIFF_W), BF16),
        compiler_params=_params("parallel", "parallel", "parallel"),
        name="diff_attention",
    )(slope_rows, lam_vecs, norm_w[layer].reshape(1, DIFF_DV), tail, tail, tail)


def _merge_kernel(oa_ref, ob_ref, oc_ref, wa_ref, wb_ref, wc_ref, ga_ref, gb_ref, gc_ref, o_ref, wbf_ref):
    @pl.when(pl.program_id(1) == 0)
    def _convert_weight_tiles():
        wbf_ref[0] = wa_ref[...].astype(BF16)
        wbf_ref[1] = wb_ref[...].astype(BF16)
        wbf_ref[2] = wc_ref[...].astype(BF16)

    acc = _sigmoid(ga_ref[...]) * _dot(oa_ref[...], wbf_ref[0])
    acc = acc + _sigmoid(gb_ref[...]) * _dot(ob_ref[...], wbf_ref[1])
    acc = acc + _sigmoid(gc_ref[...]) * _dot(oc_ref[...], wbf_ref[2])
    o_ref[...] = acc.astype(o_ref.dtype)


def branch_merge(o_a, o_b, o_c, w_branch, tail, layer, *, bm=1024, bn=512):
    m, kw = o_a.shape
    assert kw == RET_W == GDN_W == DIFF_W
    gate0 = 3 * DIFF_W // bn

    def o_spec():
        return pl.BlockSpec((bm, kw), lambda j, i: (i, 0))

    def w_spec(part):
        return pl.BlockSpec((None, kw, bn), lambda j, i: (layer, part, j))

    def g_spec(part):
        return pl.BlockSpec((bm, bn), lambda j, i: (i, gate0 + part * (D_MODEL // bn) + j))

    return pl.pallas_call(
        _merge_kernel,
        grid=(D_MODEL // bn, m // bm),
        in_specs=[o_spec(), o_spec(), o_spec(), w_spec(0), w_spec(1), w_spec(2), g_spec(0), g_spec(1), g_spec(2)],
        out_specs=pl.BlockSpec((bm, bn), lambda j, i: (i, j)),
        out_shape=jax.ShapeDtypeStruct((m, D_MODEL), BF16),
        scratch_shapes=[pltpu.VMEM((3, kw, bn), BF16)],
        compiler_params=_params("arbitrary", "arbitrary"),
        name="branch_merge",
    )(o_a, o_b, o_c, w_branch, w_branch, w_branch, tail, tail, tail)


CROSS_TQ = 512


def _cross_attn_kernel(q_ref, k_ref, v_ref, o_ref):
    scale = MEM_DH ** -0.5
    outs = []
    for hh in range(MEM_HEADS):
        sl = slice(hh * MEM_DH, (hh + 1) * MEM_DH)
        sc = _dot_nt(q_ref[:, sl], k_ref[:, sl]) * scale
        sc = sc - jnp.max(sc, axis=-1, keepdims=True)
        p = jnp.exp(sc)
        o = _dot(p.astype(BF16), v_ref[:, sl])
        outs.append(o / jnp.sum(p, axis=-1, keepdims=True))
    o_ref[...] = jnp.concatenate(outs, axis=1).astype(o_ref.dtype)


def cross_attention(q, kv, batch, seq, mem_len):
    w = MEM_HEADS * MEM_DH
    nq = seq // CROSS_TQ
    return pl.pallas_call(
        _cross_attn_kernel,
        grid=(batch, nq),
        in_specs=[
            pl.BlockSpec((CROSS_TQ, w), lambda b, i: (b * nq + i, 0)),
            pl.BlockSpec((mem_len, w), lambda b, i: (b, 0)),
            pl.BlockSpec((mem_len, w), lambda b, i: (b, 1)),
        ],
        out_specs=pl.BlockSpec((CROSS_TQ, w), lambda b, i: (b * nq + i, 0)),
        out_shape=jax.ShapeDtypeStruct((batch * seq, w), BF16),
        compiler_params=_params("parallel", "parallel"),
        name="cross_attention",
    )(q, kv, kv)


def _ffn_kernel(h_ref, wg_ref, wu_ref, wd_ref, x_ref, o_ref):
    j = pl.program_id(1)
    h = h_ref[...]
    act = (_silu(_dot(h, wg_ref[...])) * _dot(h, wu_ref[...])).astype(BF16)
    contrib = _dot(act, wd_ref[...])

    @pl.when(j == 0)
    def _first():
        o_ref[...] = x_ref[...] + contrib

    @pl.when(j > 0)
    def _rest():
        o_ref[...] += contrib


def dense_ffn(h, w_gu, w_down, x, *, bm=512, tf=512):
    m, d = h.shape
    f = w_down.shape[0]
    nf = f // tf
    assert f % tf == 0 and m % bm == 0
    return pl.pallas_call(
        _ffn_kernel,
        grid=(m // bm, nf),
        in_specs=[
            pl.BlockSpec((bm, d), lambda i, j: (i, 0)),
            pl.BlockSpec((d, tf), lambda i, j: (0, j)),
            pl.BlockSpec((d, tf), lambda i, j: (0, nf + j)),
            pl.BlockSpec((tf, d), lambda i, j: (j, 0)),
            pl.BlockSpec((bm, d), lambda i, j: (i, 0)),
        ],
        out_specs=pl.BlockSpec((bm, d), lambda i, j: (i, 0)),
        out_shape=jax.ShapeDtypeStruct((m, d), F32),
        compiler_params=_params("parallel", "arbitrary"),
        name="dense_swiglu",
    )(h, w_gu, w_gu, w_down, x)


def _router_kernel(x_ref, nw_ref, wr_ref, o_ref):
    x = x_ref[...]
    h = x * lax.rsqrt(jnp.mean(x * x, axis=-1, keepdims=True) + NORM_EPS) * nw_ref[...]
    logits = _dot_f32ish(h, wr_ref[...])
    lane = lax.broadcasted_iota(jnp.int32, logits.shape, 1)
    logits = jnp.where(lane < N_EXPERTS, logits, -jnp.inf)
    m1 = jnp.max(logits, axis=-1, keepdims=True)
    i1 = jnp.min(jnp.where(logits == m1, lane, V7X_LANES), axis=-1, keepdims=True)
    rest = jnp.where(lane == i1, -jnp.inf, logits)
    m2 = jnp.max(rest, axis=-1, keepdims=True)
    i2 = jnp.min(jnp.where(rest == m2, lane, V7X_LANES), axis=-1, keepdims=True)
    e2 = jnp.exp(m2 - m1)
    w1 = 1.0 / (1.0 + e2)
    w2 = e2 / (1.0 + e2)
    out = jnp.where(lane == 0, i1.astype(F32), 0.0)
    out = jnp.where(lane == 1, i2.astype(F32), out)
    out = jnp.where(lane == 2, w1, out)
    out = jnp.where(lane == 3, w2, out)
    o_ref[...] = out


def moe_router(x, norm_w, w_router, *, bm=512):
    t, d = x.shape
    wr = jnp.zeros((d, V7X_LANES), F32).at[:, :N_EXPERTS].set(w_router)
    return pl.pallas_call(
        _router_kernel,
        grid=(t // bm,),
        in_specs=[pl.BlockSpec((bm, d), lambda i: (i, 0)), pl.BlockSpec((1, d), lambda i: (0, 0)),
                  pl.BlockSpec((d, V7X_LANES), lambda i: (0, 0))],
        out_specs=pl.BlockSpec((bm, V7X_LANES), lambda i: (i, 0)),
        out_shape=jax.ShapeDtypeStruct((t, V7X_LANES), F32),
        compiler_params=_params("parallel"),
        name="moe_router",
    )(x, norm_w.reshape(1, d), wr)


MOE_BM = 1024
MOE_TF = 256


def _moe_ffn_kernel(te_ref, ts_ref, tv_ref, x_ref, wg_ref, wu_ref, wd_ref, o_ref):
    t, j = pl.program_id(0), pl.program_id(1)
    valid = tv_ref[t] == 1

    @pl.when(valid)
    def _compute():
        x = x_ref[...]
        a = _dot(x, wg_ref[...].astype(BF16))
        u = _dot(x, wu_ref[...].astype(BF16))
        contrib = _dot((_silu(a) * u).astype(BF16), wd_ref[...].astype(BF16))

        @pl.when(j == 0)
        def _first():
            o_ref[...] = contrib

        @pl.when(j > 0)
        def _rest():
            o_ref[...] += contrib

    @pl.when(jnp.logical_not(valid) & (j == 0))
    def _unused_tile():
        o_ref[...] = jnp.zeros_like(o_ref)


def moe_experts(xs, w_gu, w_down, moe_layer, tile_expert, tile_src, tile_valid):
    rows, d = xs.shape
    bm, tf = MOE_BM, MOE_TF
    f = w_down.shape[-2]
    nf = f // tf
    nt = rows // bm

    def jeff(j, tv, t):
        return jnp.where(tv[t] == 1, j, nf - 1)

    grid_spec = pltpu.PrefetchScalarGridSpec(
        num_scalar_prefetch=3,
        grid=(nt, nf),
        in_specs=[
            pl.BlockSpec((bm, d), lambda t, j, te, ts, tv: (ts[t], 0)),
            pl.BlockSpec((None, None, d, tf), lambda t, j, te, ts, tv: (moe_layer, te[t], 0, jeff(j, tv, t))),
            pl.BlockSpec((None, None, d, tf), lambda t, j, te, ts, tv: (moe_layer, te[t], 0, nf + jeff(j, tv, t))),
            pl.BlockSpec((None, None, tf, d), lambda t, j, te, ts, tv: (moe_layer, te[t], jeff(j, tv, t), 0)),
        ],
        out_specs=pl.BlockSpec((bm, d), lambda t, j, te, ts, tv: (t, 0)),
    )
    return pl.pallas_call(
        _moe_ffn_kernel,
        grid_spec=grid_spec,
        out_shape=jax.ShapeDtypeStruct((rows, d), F32),
        compiler_params=_params("arbitrary", "arbitrary"),
        name="moe_experts",
    )(tile_expert, tile_src, tile_valid, xs, w_gu, w_gu, w_down)


def _combine_kernel(x_ref, r_ref, y1_ref, y2_ref, o_ref):
    r = r_ref[...]
    o_ref[...] = x_ref[...] + r[:, 2:3] * y1_ref[...] + r[:, 3:4] * y2_ref[...]


def moe_combine(x, route, y1, y2, *, bm=512):
    t, d = x.shape
    big = pl.BlockSpec((bm, d), lambda i: (i, 0))
    return pl.pallas_call(
        _combine_kernel,
        grid=(t // bm,),
        in_specs=[big, pl.BlockSpec((bm, V7X_LANES), lambda i: (i, 0)), big, big],
        out_specs=big,
        out_shape=jax.ShapeDtypeStruct((t, d), F32),
        compiler_params=_params("parallel"),
        name="moe_combine",
    )(x, route, y1, y2)


def moe_layer_fn(x, h, norm_w, w_router, w_gu, w_down, moe_layer):
    t, d = x.shape
    bm = MOE_BM
    route = moe_router(x, norm_w, w_router)
    expert = route[:, :TOP_K].astype(jnp.int32).reshape(-1)
    n_assign = t * TOP_K
    nt = n_assign // bm + N_EXPERTS
    onehot = (expert[:, None] == jnp.arange(N_EXPERTS)[None, :]).astype(jnp.int32)
    csum = jnp.cumsum(onehot, axis=0)
    rank = jnp.sum(onehot * csum, axis=1) - 1
    counts = csum[-1]
    tiles_per = (counts + bm - 1) // bm
    tile_end = jnp.cumsum(tiles_per)
    tile_start = tile_end - tiles_per
    pos = tile_start[expert] * bm + rank
    n_valid = tile_end[-1]
    tile_ids = jnp.arange(nt, dtype=jnp.int32)
    tile_src = jnp.minimum(tile_ids, n_valid - 1).astype(jnp.int32)
    tile_expert = jnp.minimum(jnp.sum(tile_src[:, None] >= tile_end[None, :], axis=1), N_EXPERTS - 1).astype(jnp.int32)
    tile_valid = (tile_ids < n_valid).astype(jnp.int32)
    row_token = jnp.zeros((nt * bm,), jnp.int32).at[pos].set(jnp.arange(n_assign, dtype=jnp.int32) // TOP_K)
    xs = jnp.take(h, row_token, axis=0)
    ys = moe_experts(xs, w_gu, w_down, moe_layer, tile_expert, tile_src, tile_valid)
    pos2 = pos.reshape(t, TOP_K)
    y1 = jnp.take(ys, pos2[:, 0], axis=0)
    y2 = jnp.take(ys, pos2[:, 1], axis=0)
    return moe_combine(x, route, y1, y2)


def kernel(x, mem, w_in, conv_w, gdn_a_log, gdn_dt_bias, gdn_norm_w, diff_lambda, diff_norm_w, w_branch, w_out,
           mix_norm_w, mem_norm_w, cross_norm_w, cross_w_q, cross_w_kv, cross_w_o, ffn_norm_w, dense_w_gu,
           dense_w_down, router_w, expert_w_gu, expert_w_down, final_norm_w):
    batch, seq, d = x.shape
    mem_len = mem.shape[1]
    t = batch * seq
    x = x.reshape(t, d)
    mem2 = mem.reshape(batch * mem_len, d)
    for l in range(DEPTH):
        lambda_init = 0.8 - 0.6 * math.exp(-0.3 * l)
        h = rmsnorm(x, mix_norm_w[l], BF16)
        proj = matmul(h, w_in, (l,), 0, OFF_SMALL, bm=1024, bn=1024, out_dtype=F32, name="in_proj_ab")
        w_small = jnp.zeros((d, V7X_LANES), F32).at[:, :2 * GDN_HEADS].set(w_in[l, :, OFF_SMALL:OFF_DIFF])
        small = matmul(h, w_small, (), 0, V7X_LANES, bm=1024, bn=V7X_LANES, out_dtype=F32, name="in_proj_small")
        w_tail = w_in[l, :, OFF_DIFF:]
        tail = matmul(h, w_tail, (), 0, TAIL_W, bm=1024, bn=1024, out_dtype=F32, name="in_proj_tail")
        o_a = retention_mixer(proj, batch, seq)
        o_b = gdn_mixer(proj, small, conv_w, gdn_a_log, gdn_dt_bias, gdn_norm_w, l, batch, seq)
        o_c = diff_attn_mixer(tail, diff_lambda, diff_norm_w, l, lambda_init, batch, seq)
        merged = branch_merge(o_a, o_b, o_c, w_branch, tail, l)
        x = matmul(merged, w_out, (l,), 0, d, bm=1024, bn=1024, out_dtype=F32, res=x, name="out_proj")
        h = rmsnorm(x, cross_norm_w[l], BF16)
        m_n = rmsnorm(mem2, mem_norm_w[l], BF16)
        q = matmul(h, cross_w_q, (l,), 0, MEM_HEADS * MEM_DH, bm=1024, bn=512, out_dtype=BF16, name="cross_q")
        kv = matmul(m_n, cross_w_kv, (l,), 0, 2 * MEM_HEADS * MEM_DH, bm=1024, bn=512, out_dtype=BF16,
                    name="cross_kv")
        att = cross_attention(q, kv, batch, seq, mem_len)
        x = matmul(att, cross_w_o, (l,), 0, d, bm=1024, bn=1024, out_dtype=F32, res=x, name="cross_o")
        if l % 2 == 0:
            h = rmsnorm(x, ffn_norm_w[l], BF16)
            x = dense_ffn(h, dense_w_gu[l // 2].astype(BF16), dense_w_down[l // 2].astype(BF16), x)
        else:
            h = rmsnorm(x, ffn_norm_w[l], BF16)
            x = moe_layer_fn(x, h, ffn_norm_w[l], router_w[l // 2], expert_w_gu, expert_w_down, l // 2)
    out = rmsnorm(x, final_norm_w, F32)
    return out.reshape(batch, seq, d)
```

```python
import functools
import math

import jax
import jax.numpy as jnp
from jax import lax
from jax.experimental import pallas as pl
from jax.experimental.pallas import tpu as pltpu

F32 = jnp.float32
BF16 = jnp.bfloat16
U32 = jnp.uint32

D_MODEL = 2048
DEPTH = 2
NORM_EPS = 1e-6
RET_HEADS, RET_DK, RET_CHUNK = 8, 128, 128
GDN_HEADS, GDN_DK, GDN_CONV, GDN_CHUNK = 8, 128, 4, 64
DIFF_HEADS, DIFF_DK, DIFF_DV = 4, 128, 256
MEM_HEADS, MEM_DH = 4, 128
D_FF = 5632
N_EXPERTS, TOP_K, EXPERT_FF = 8, 2, 7168

RET_W = RET_HEADS * RET_DK
GDN_W = GDN_HEADS * GDN_DK
DIFF_W = DIFF_HEADS * DIFF_DV
OFF_GDN = 4 * RET_W
OFF_SMALL = OFF_GDN + 4 * GDN_W
OFF_DIFF = OFF_SMALL + 2 * GDN_HEADS
TAIL_W = 3 * DIFF_W + 3 * D_MODEL

V7X_LANES = 128
V7X_MXU_DIM = 256
V7X_VMEM_BYTES = 64 * 1024 * 1024
VMEM_LIMIT = (V7X_VMEM_BYTES * 3) // 4
VMEM_LIMIT_LARGE = (V7X_VMEM_BYTES * 7) // 8

NEG_BIG = -1e30


def _params(*semantics, vmem=VMEM_LIMIT):
    return pltpu.CompilerParams(dimension_semantics=semantics, vmem_limit_bytes=vmem)


def _sigmoid(x):
    return 1.0 / (1.0 + jnp.exp(-x))


def _silu(x):
    return x * _sigmoid(x)


def _dot(a, b):
    return jnp.dot(a, b, preferred_element_type=F32)


def _dot_nt(a, b):
    return lax.dot_general(a, b, (((1,), (1,)), ((), ())), preferred_element_type=F32)


def _dot_tn(a, b):
    return lax.dot_general(a, b, (((0,), (0,)), ((), ())), preferred_element_type=F32)


def _split_bf16(x):
    hi = x.astype(BF16)
    lo = (x - hi.astype(F32)).astype(BF16)
    return hi, lo


def _dot_f32ish(a, b):
    ah, al = _split_bf16(a)
    bh, bl = _split_bf16(b)
    lhs = jnp.concatenate([ah, ah, al], axis=1)
    rhs = jnp.concatenate([bh, bl, bh], axis=0)
    return _dot(lhs, rhs)


def _rms_normalize(x):
    return x * lax.rsqrt(jnp.mean(x * x, axis=-1, keepdims=True) + NORM_EPS)


def _rmsnorm_kernel(x_ref, w_ref, o_ref):
    o_ref[...] = (_rms_normalize(x_ref[...]) * w_ref[...]).astype(o_ref.dtype)


def rmsnorm(x, w, out_dtype, bm=512):
    m, d = x.shape
    bm = min(bm, m)
    return pl.pallas_call(
        _rmsnorm_kernel,
        grid=(m // bm,),
        in_specs=[pl.BlockSpec((bm, d), lambda i: (i, 0)), pl.BlockSpec((1, d), lambda i: (0, 0))],
        out_specs=pl.BlockSpec((bm, d), lambda i: (i, 0)),
        out_shape=jax.ShapeDtypeStruct((m, d), out_dtype),
        compiler_params=_params("parallel"),
        name="rmsnorm",
    )(x, w.reshape(1, d))


def _matmul_kernel(x_ref, w_ref, *rest, has_res):
    if has_res:
        res_ref, o_ref, wbf_ref = rest
    else:
        o_ref, wbf_ref = rest

    @pl.when(pl.program_id(1) == 0)
    def _convert_weight_tile():
        wbf_ref[...] = w_ref[...].astype(BF16)

    acc = _dot(x_ref[...].astype(BF16), wbf_ref[...])
    if has_res:
        acc = acc + res_ref[...]
    o_ref[...] = acc.astype(o_ref.dtype)


def matmul(x, w, w_lead, col_block0, n_out, *, bm, bn, out_dtype, res=None, name="matmul"):
    m, k = x.shape
    bm = min(bm, m)
    bn = min(bn, n_out)
    assert m % bm == 0 and n_out % bn == 0 and w.shape[-2] == k
    n_lead = len(w_lead)
    in_specs = [
        pl.BlockSpec((bm, k), lambda j, i: (i, 0)),
        pl.BlockSpec((None,) * n_lead + (k, bn), lambda j, i: (*w_lead, 0, col_block0 + j)),
    ]
    args = [x, w]
    if res is not None:
        in_specs.append(pl.BlockSpec((bm, bn), lambda j, i: (i, j)))
        args.append(res)
    return pl.pallas_call(
        functools.partial(_matmul_kernel, has_res=res is not None),
        grid=(n_out // bn, m // bm),
        in_specs=in_specs,
        out_specs=pl.BlockSpec((bm, bn), lambda j, i: (i, j)),
        out_shape=jax.ShapeDtypeStruct((m, n_out), out_dtype),
        scratch_shapes=[pltpu.VMEM((k, bn), BF16)],
        compiler_params=_params("arbitrary", "arbitrary"),
        name=name,
    )(*args)


SHIFT_ROWS = 256


def _matmul_shifted_kernel(x_ref, w_ref, wnext_ref, o_ref, wbf_ref, *, shift):
    @pl.when(pl.program_id(1) == 0)
    def _realign_and_convert_weight_tile():
        k, bn = w_ref.shape
        for r0 in range(0, k, SHIFT_ROWS):
            both = jnp.concatenate([w_ref[r0:r0 + SHIFT_ROWS, :], wnext_ref[r0:r0 + SHIFT_ROWS, :]], axis=1)
            wbf_ref[r0:r0 + SHIFT_ROWS, :] = both[:, shift:shift + bn].astype(BF16)

    o_ref[...] = _dot(x_ref[...], wbf_ref[...]).astype(o_ref.dtype)


def matmul_shifted(x, w, layer, col0, n_out, *, bm, bn, out_dtype, name):
    m, k = x.shape
    aligned0 = (col0 // V7X_LANES) * V7X_LANES
    shift = col0 - aligned0
    assert 0 < shift < V7X_LANES and aligned0 % bn == 0 and n_out % bn == 0 and m % bm == 0 and k % SHIFT_ROWS == 0
    lanes_per_tile = bn // V7X_LANES
    return pl.pallas_call(
        functools.partial(_matmul_shifted_kernel, shift=shift),
        grid=(n_out // bn, m // bm),
        in_specs=[
            pl.BlockSpec((bm, k), lambda j, i: (i, 0)),
            pl.BlockSpec((None, k, bn), lambda j, i: (layer, 0, aligned0 // bn + j)),
            pl.BlockSpec((None, k, V7X_LANES), lambda j, i: (layer, 0, aligned0 // V7X_LANES + (j + 1) * lanes_per_tile)),
        ],
        out_specs=pl.BlockSpec((bm, bn), lambda j, i: (i, j)),
        out_shape=jax.ShapeDtypeStruct((m, n_out), out_dtype),
        scratch_shapes=[pltpu.VMEM((k, bn), BF16)],
        compiler_params=_params("arbitrary", "arbitrary"),
        name=name,
    )(x, w, w)


RET_HEADS_PER_STEP = 2


def _retention_kernel(lg_ref, q_ref, k_ref, v_ref, g_ref, o_ref):
    s = q_ref.shape[0]
    c, dk, nh = RET_CHUNK, RET_DK, RET_HEADS_PER_STEP
    row = lax.broadcasted_iota(jnp.int32, (c, c), 0).astype(F32)
    col = lax.broadcasted_iota(jnp.int32, (c, c), 1).astype(F32)
    diff = row - col
    causal = diff >= 0
    scale = dk ** -0.5
    consts = []
    for hh in range(nh):
        lg = lg_ref[hh]
        decay = jnp.where(causal, jnp.exp(lg * jnp.where(causal, diff, 0.0)), 0.0) * scale
        to_end = jnp.exp(lg * (c - 1.0 - row)) * scale
        from_start = jnp.exp(lg * (row + 1.0))
        chunk_decay = jnp.exp(lg * float(c))
        consts.append((decay, to_end, from_start, chunk_decay))

    def body(i, states):
        r = pl.ds(pl.multiple_of(i * c, c), c)
        new_states = []
        for hh in range(nh):
            decay, to_end, from_start, chunk_decay = consts[hh]
            cs = slice(hh * dk, (hh + 1) * dk)
            kf = k_ref[r, cs]
            qc = q_ref[r, cs].astype(BF16)
            vc = v_ref[r, cs].astype(BF16)
            inner = _dot_nt(qc, kf.astype(BF16)) * decay
            o = _dot(inner.astype(BF16), vc)
            o = o + _dot(qc, states[hh].astype(BF16)) * from_start
            new_states.append(chunk_decay * states[hh] + _dot_tn((kf * to_end).astype(BF16), vc))
            mu = jnp.mean(o, axis=-1, keepdims=True)
            var = jnp.mean(jnp.square(o - mu), axis=-1, keepdims=True)
            on = (o - mu) * lax.rsqrt(var + NORM_EPS)
            o_ref[r, cs] = (on * _silu(g_ref[r, cs])).astype(o_ref.dtype)
        return tuple(new_states)

    init = tuple(jnp.zeros((dk, dk), F32) for _ in range(nh))
    lax.fori_loop(0, s // c, body, init, unroll=2)


def retention_mixer(proj, batch, seq):
    assert RET_DK == RET_CHUNK == V7X_LANES and RET_HEADS % RET_HEADS_PER_STEP == 0
    nh = RET_HEADS_PER_STEP
    steps = RET_HEADS // nh
    log_g = jnp.log(1.0 - 2.0 ** (-5.0 - jnp.arange(RET_HEADS, dtype=F32)))
    lg = jnp.broadcast_to(log_g[:, None, None], (RET_HEADS, 1, V7X_LANES))

    def col(base):
        return pl.BlockSpec((seq, nh * RET_DK), lambda b, hh: (b, base * steps + hh))

    return pl.pallas_call(
        _retention_kernel,
        grid=(batch, steps),
        in_specs=[pl.BlockSpec((nh, 1, V7X_LANES), lambda b, hh: (hh, 0, 0)), col(0), col(1), col(2), col(3)],
        out_specs=pl.BlockSpec((seq, nh * RET_DK), lambda b, hh: (b, hh)),
        out_shape=jax.ShapeDtypeStruct((batch * seq, RET_W), BF16),
        compiler_params=_params("parallel", "parallel"),
        name="retention",
    )(lg, proj, proj, proj, proj)


GDN_GROUP = V7X_MXU_DIM
GDN_HEADS_PER_STEP = 2
GDN_UNROLL = 1


def _causal_conv_silu(x, w):
    row = lax.broadcasted_iota(jnp.int32, x.shape, 0)
    y = x * w[GDN_CONV - 1:GDN_CONV, :]
    for sh in range(1, GDN_CONV):
        xs = jnp.where(row >= sh, pltpu.roll(x, sh, 0), 0.0)
        y = y + xs * w[GDN_CONV - 1 - sh:GDN_CONV - sh, :]
    return _silu(y)


def _l2_normalize(t):
    return t * lax.rsqrt(jnp.sum(t * t, axis=-1, keepdims=True) + 1e-6)


def _gdn_kernel(xq_ref, xk_ref, xv_ref, z_ref, sm_ref, cwq_ref, cwk_ref, cwv_ref, alog_ref, dtb_ref, nw_ref,
                o_ref, q_s, k_s, v_s, g_s, b_s):
    s = xq_ref.shape[0]
    dk, nh = GDN_DK, GDN_HEADS_PER_STEP
    head0 = pl.program_id(1) * nh
    gsz, c = GDN_GROUP, GDN_CHUNK
    n_sub = gsz // c

    qa = _causal_conv_silu(xq_ref[...], cwq_ref[...])
    ka = _causal_conv_silu(xk_ref[...], cwk_ref[...])
    v_s[...] = _causal_conv_silu(xv_ref[...], cwv_ref[...])
    sm = sm_ref[...]
    lane = lax.broadcasted_iota(jnp.int32, sm.shape, 1)
    z_dec = sm + dtb_ref[...]
    softplus = jnp.maximum(z_dec, 0.0) + jnp.log(1.0 + jnp.exp(-jnp.abs(z_dec)))
    g_all = -jnp.exp(alog_ref[...]) * softplus
    beta_all = _sigmoid(sm)
    for hh in range(nh):
        cs = slice(hh * dk, (hh + 1) * dk)
        q_s[:, cs] = _l2_normalize(qa[:, cs]) * (dk ** -0.5)
        k_s[:, cs] = _l2_normalize(ka[:, cs])
        g_col = jnp.sum(jnp.where(lane == head0 + hh + GDN_HEADS, g_all, 0.0), axis=-1, keepdims=True)
        b_col = jnp.sum(jnp.where(lane == head0 + hh, beta_all, 0.0), axis=-1, keepdims=True)
        g_s[:, cs] = jnp.broadcast_to(g_col, (s, dk))
        b_s[:, cs] = jnp.broadcast_to(b_col, (s, dk))

    ri = lax.broadcasted_iota(jnp.int32, (gsz, gsz), 0)
    ci = lax.broadcasted_iota(jnp.int32, (gsz, gsz), 1)
    shift = int(math.log2(c))
    same = jnp.right_shift(ri, shift) == jnp.right_shift(ci, shift)
    incl = same & (ri >= ci)
    strict = same & (ri > ci)
    cum_mat = jnp.concatenate([jnp.where(incl, 1.0, 0.0), jnp.where(same, 1.0, 0.0)], axis=0).astype(BF16)
    nw = nw_ref[...]

    def split3(x):
        x1 = x.astype(BF16)
        r1 = x - x1.astype(F32)
        x2 = r1.astype(BF16)
        x3 = (r1 - x2.astype(F32)).astype(BF16)
        return jnp.concatenate([x1, x2, x3], axis=1)

    def group(gi, states):
        r = pl.ds(pl.multiple_of(gi * gsz, gsz), gsz)
        heads = range(nh)
        cols = [slice(hh * dk, (hh + 1) * dk) for hh in heads]
        q = [q_s[r, cs] for cs in cols]
        k = [k_s[r, cs] for cs in cols]
        v = [v_s[r, cs] for cs in cols]
        bb = [b_s[r, cs] for cs in cols]
        cums = [_dot(cum_mat, split3(g_s[r, cs])) for cs in cols]
        cums = [t[:, :dk] + t[:, dk:2 * dk] + t[:, 2 * dk:] for t in cums]
        gc = [t[:gsz] for t in cums]
        g_last = [t[gsz:] for t in cums]
        kb = [t.astype(BF16) for t in k]
        kk = [_dot_nt(t, t) for t in kb]
        qk = [_dot_nt(q[h].astype(BF16), kb[h]) for h in heads]
        decay_l = []
        for h in heads:
            gc2 = jnp.concatenate([gc[h], gc[h]], axis=1)
            seg = gc2 - gc2.T
            decay_l.append(jnp.where(incl, jnp.exp(jnp.where(incl, seg, 0.0)), 0.0))
        a = [jnp.where(strict, kk[h] * decay_l[h] * jnp.concatenate([bb[h], bb[h]], axis=1), 0.0) for h in heads]
        ab = [t.astype(BF16) for t in a]
        pw = [_dot(t, t) for t in ab]
        rr = [-t for t in a]
        n_round = int(math.log2(c)) - 1
        for it in range(n_round):
            pwb = [t.astype(BF16) for t in pw]
            rr_next = [rr[h] + pw[h] + _dot(rr[h].astype(BF16), pwb[h]) for h in heads]
            if it + 1 < n_round:
                pw = [_dot(t, t) for t in pwb]
            rr = rr_next
        eg = [jnp.exp(t) for t in gc]
        rhs = [jnp.concatenate([k[h] * (bb[h] * eg[h]), v[h] * bb[h]], axis=1) for h in heads]
        wu = [rhs[h] + _dot(rr[h].astype(BF16), rhs[h].astype(BF16)) for h in heads]
        wub = [t.astype(BF16) for t in wu]
        a_qk = [(qk[h] * decay_l[h]).astype(BF16) for h in heads]
        k_tail = [(k[h] * jnp.exp(g_last[h] - gc[h])).astype(BF16) for h in heads]
        e_last = [jnp.exp(t) for t in g_last]
        aw = [_dot(a_qk[h], wub[h]) for h in heads]
        q_eff = [(q[h] * eg[h] - aw[h][:, :dk]).astype(BF16) for h in heads]
        kt = [[_dot_tn(k_tail[h][i * c:(i + 1) * c], wub[h][i * c:(i + 1) * c]) for i in range(n_sub)] for h in heads]
        states = list(states)
        outs = [[] for _ in heads]
        for i in range(n_sub):
            lo, hi = i * c, (i + 1) * c
            sb = [t.astype(BF16) for t in states]
            for h in heads:
                outs[h].append(_dot(q_eff[h][lo:hi], sb[h]) + aw[h][lo:hi, dk:])
            states = [states[h] * e_last[h][lo:lo + 1, :] - _dot(kt[h][i][:, :dk].astype(BF16), sb[h]) + kt[h][i][:, dk:]
                      for h in heads]
        for h in heads:
            o = jnp.concatenate(outs[h], axis=0)
            o_ref[r, cols[h]] = (_rms_normalize(o) * nw * _silu(z_ref[r, cols[h]])).astype(o_ref.dtype)
        return tuple(states)

    init = tuple(jnp.zeros((dk, dk), F32) for _ in range(nh))
    lax.fori_loop(0, s // gsz, group, init, unroll=GDN_UNROLL)


def gdn_mixer(proj, small, conv_w, a_log, dt_bias, norm_w, layer, batch, seq):
    assert GDN_DK == V7X_LANES and GDN_GROUP == 2 * GDN_DK and GDN_GROUP % GDN_CHUNK == 0
    assert seq % (GDN_UNROLL * GDN_GROUP) == 0
    h, nh = GDN_HEADS, GDN_HEADS_PER_STEP
    steps = h // nh
    wblk = nh * GDN_DK
    base = OFF_GDN // wblk
    pad = V7X_LANES - 2 * h
    alog_row = jnp.concatenate([jnp.zeros((h,), F32), a_log[layer], jnp.zeros((pad,), F32)]).reshape(1, V7X_LANES)
    dtb_row = jnp.concatenate([jnp.zeros((h,), F32), dt_bias[layer], jnp.zeros((pad,), F32)]).reshape(1, V7X_LANES)

    def col(j):
        return pl.BlockSpec((seq, wblk), lambda b, hh: (b, base + j * steps + hh))

    def cw(j):
        return pl.BlockSpec((None, GDN_CONV, wblk), lambda b, hh: (layer, 0, j * steps + hh))

    row_spec = pl.BlockSpec((1, V7X_LANES), lambda b, hh: (0, 0))
    return pl.pallas_call(
        _gdn_kernel,
        grid=(batch, steps),
        in_specs=[col(0), col(1), col(2), col(3),
                  pl.BlockSpec((seq, V7X_LANES), lambda b, hh: (b, 0)),
                  cw(0), cw(1), cw(2), row_spec, row_spec, row_spec],
        out_specs=pl.BlockSpec((seq, wblk), lambda b, hh: (b, hh)),
        out_shape=jax.ShapeDtypeStruct((batch * seq, GDN_W), BF16),
        scratch_shapes=[pltpu.VMEM((seq, wblk), F32)] * 5,
        compiler_params=_params("parallel", "parallel"),
        name="gated_deltanet",
    )(proj, proj, proj, proj, small, conv_w, conv_w, conv_w, alog_row, dtb_row, norm_w[layer].reshape(1, GDN_DK))


DIFF_TQ = 512
DIFF_TK = 512


def _diff_attn_kernel(slope_ref, lam_ref, nw_ref, q_ref, k_ref, v_ref, o_ref, *, out_scale, lambda_init):
    tq, tk, dk = DIFF_TQ, DIFF_TK, DIFF_DK
    qi = pl.program_id(2)
    slope = slope_ref[...][:, :1]
    lv = lam_ref[...]
    lam = (jnp.exp(jnp.sum(lv[0:1] * lv[1:2], axis=-1, keepdims=True))
           - jnp.exp(jnp.sum(lv[2:3] * lv[3:4], axis=-1, keepdims=True)) + lambda_init)
    q = q_ref[...] * (dk ** -0.5)
    q1, q2 = q[:, :dk].astype(BF16), q[:, dk:].astype(BF16)
    key_bias0 = slope * lax.broadcasted_iota(jnp.int32, (1, tk), 1).astype(F32)
    on_or_below_diag = lax.broadcasted_iota(jnp.int32, (tq, tk), 0) >= lax.broadcasted_iota(jnp.int32, (tq, tk), 1)

    def block(j, carry, masked):
        m1, l1, a1, m2, l2, a2 = carry
        r = pl.ds(pl.multiple_of(j * tk, tk), tk)
        kb = k_ref[r, :].astype(BF16)
        vb = v_ref[r, :].astype(BF16)
        key_bias = key_bias0 + slope * (j * tk).astype(F32)

        def update(qh, kh, m, l, acc):
            sc = _dot_nt(qh, kh) + key_bias
            if masked:
                sc = jnp.where(on_or_below_diag, sc, NEG_BIG)
            m_new = jnp.maximum(m, jnp.max(sc, axis=-1, keepdims=True))
            alpha = jnp.exp(m - m_new)
            p = jnp.exp(sc - m_new)
            l = alpha * l + jnp.sum(p, axis=-1, keepdims=True)
            acc = alpha * acc + _dot(p.astype(BF16), vb)
            return m_new, l, acc

        m1, l1, a1 = update(q1, kb[:, :dk], m1, l1, a1)
        m2, l2, a2 = update(q2, kb[:, dk:], m2, l2, a2)
        return m1, l1, a1, m2, l2, a2

    neg = jnp.full((tq, 1), NEG_BIG, F32)
    zero = jnp.zeros((tq, 1), F32)
    zacc = jnp.zeros((tq, DIFF_DV), F32)
    carry = lax.fori_loop(0, qi, lambda j, cr: block(j, cr, False), (neg, zero, zacc, neg, zero, zacc))
    m1, l1, a1, m2, l2, a2 = block(qi, carry, True)
    o = a1 / l1 - lam * (a2 / l2)
    o_ref[...] = (_rms_normalize(o) * nw_ref[...] * out_scale).astype(o_ref.dtype)


def diff_attn_mixer(tail, lam_vecs, norm_w, layer, lambda_init, batch, seq):
    assert DIFF_TQ == DIFF_TK and seq % DIFF_TQ == 0 and 2 * DIFF_DK == DIFF_DV
    h = DIFF_HEADS
    nq = seq // DIFF_TQ
    slopes = 2.0 ** (-8.0 * (jnp.arange(h, dtype=F32) + 1.0) / h)
    slope_rows = jnp.broadcast_to(slopes[:, None, None], (h, 1, V7X_LANES))
    kernel = functools.partial(_diff_attn_kernel, out_scale=1.0 - lambda_init, lambda_init=lambda_init)
    return pl.pallas_call(
        kernel,
        grid=(batch, h, nq),
        in_specs=[
            pl.BlockSpec((None, 1, V7X_LANES), lambda b, hh, i: (hh, 0, 0)),
            pl.BlockSpec((None, 4, DIFF_DK), lambda b, hh, i: (layer, 0, 0)),
            pl.BlockSpec((1, DIFF_DV), lambda b, hh, i: (0, 0)),
            pl.BlockSpec((DIFF_TQ, 2 * DIFF_DK), lambda b, hh, i: (b * nq + i, hh)),
            pl.BlockSpec((seq, 2 * DIFF_DK), lambda b, hh, i: (b, h + hh)),
            pl.BlockSpec((seq, DIFF_DV), lambda b, hh, i: (b, 2 * h + hh)),
        ],
        out_specs=pl.BlockSpec((DIFF_TQ, DIFF_DV), lambda b, hh, i: (b * nq + i, hh)),
        out_shape=jax.ShapeDtypeStruct((batch * seq, DIFF_W), BF16),
        compiler_params=_params("parallel", "parallel", "parallel"),
        name="diff_attention",
    )(slope_rows, lam_vecs, norm_w[layer].reshape(1, DIFF_DV), tail, tail, tail)


def _merge_kernel(oa_ref, ob_ref, oc_ref, wa_ref, wb_ref, wc_ref, ga_ref, gb_ref, gc_ref, o_ref, wbf_ref):
    @pl.when(pl.program_id(1) == 0)
    def _convert_weight_tiles():
        wbf_ref[0] = wa_ref[...].astype(BF16)
        wbf_ref[1] = wb_ref[...].astype(BF16)
        wbf_ref[2] = wc_ref[...].astype(BF16)

    acc = _sigmoid(ga_ref[...]) * _dot(oa_ref[...], wbf_ref[0])
    acc = acc + _sigmoid(gb_ref[...]) * _dot(ob_ref[...], wbf_ref[1])
    acc = acc + _sigmoid(gc_ref[...]) * _dot(oc_ref[...], wbf_ref[2])
    o_ref[...] = acc.astype(o_ref.dtype)


def branch_merge(o_a, o_b, o_c, w_branch, tail, layer, *, bm=1024, bn=512):
    m, kw = o_a.shape
    assert kw == RET_W == GDN_W == DIFF_W
    gate0 = 3 * DIFF_W // bn

    def o_spec():
        return pl.BlockSpec((bm, kw), lambda j, i: (i, 0))

    def w_spec(part):
        return pl.BlockSpec((None, kw, bn), lambda j, i: (layer, part, j))

    def g_spec(part):
        return pl.BlockSpec((bm, bn), lambda j, i: (i, gate0 + part * (D_MODEL // bn) + j))

    return pl.pallas_call(
        _merge_kernel,
        grid=(D_MODEL // bn, m // bm),
        in_specs=[o_spec(), o_spec(), o_spec(), w_spec(0), w_spec(1), w_spec(2), g_spec(0), g_spec(1), g_spec(2)],
        out_specs=pl.BlockSpec((bm, bn), lambda j, i: (i, j)),
        out_shape=jax.ShapeDtypeStruct((m, D_MODEL), BF16),
        scratch_shapes=[pltpu.VMEM((3, kw, bn), BF16)],
        compiler_params=_params("arbitrary", "arbitrary"),
        name="branch_merge",
    )(o_a, o_b, o_c, w_branch, w_branch, w_branch, tail, tail, tail)


CROSS_TQ = 512


def _cross_attn_kernel(q_ref, k_ref, v_ref, o_ref):
    scale = MEM_DH ** -0.5
    outs = []
    for hh in range(MEM_HEADS):
        sl = slice(hh * MEM_DH, (hh + 1) * MEM_DH)
        sc = _dot_nt(q_ref[:, sl], k_ref[:, sl]) * scale
        sc = sc - jnp.max(sc, axis=-1, keepdims=True)
        p = jnp.exp(sc)
        o = _dot(p.astype(BF16), v_ref[:, sl])
        outs.append(o / jnp.sum(p, axis=-1, keepdims=True))
    o_ref[...] = jnp.concatenate(outs, axis=1).astype(o_ref.dtype)


def cross_attention(q, kv, batch, seq, mem_len):
    w = MEM_HEADS * MEM_DH
    nq = seq // CROSS_TQ
    return pl.pallas_call(
        _cross_attn_kernel,
        grid=(batch, nq),
        in_specs=[
            pl.BlockSpec((CROSS_TQ, w), lambda b, i: (b * nq + i, 0)),
            pl.BlockSpec((mem_len, w), lambda b, i: (b, 0)),
            pl.BlockSpec((mem_len, w), lambda b, i: (b, 1)),
        ],
        out_specs=pl.BlockSpec((CROSS_TQ, w), lambda b, i: (b * nq + i, 0)),
        out_shape=jax.ShapeDtypeStruct((batch * seq, w), BF16),
        compiler_params=_params("parallel", "parallel"),
        name="cross_attention",
    )(q, kv, kv)


def _ffn_kernel(h_ref, wg_ref, wu_ref, wd_ref, x_ref, o_ref):
    @pl.when(pl.program_id(1) == 0)
    def _start_from_residual():
        o_ref[...] = x_ref[...]

    h = h_ref[...]
    act = (_silu(_dot(h, wg_ref[...])) * _dot(h, wu_ref[...])).astype(BF16)
    o_ref[...] += _dot(act, wd_ref[...])


def dense_ffn(h, w_gu, w_down, x, *, bm=512, tf=512):
    m, d = h.shape
    f = w_down.shape[0]
    nf = f // tf
    assert f % tf == 0 and m % bm == 0
    return pl.pallas_call(
        _ffn_kernel,
        grid=(m // bm, nf),
        in_specs=[
            pl.BlockSpec((bm, d), lambda i, j: (i, 0)),
            pl.BlockSpec((d, tf), lambda i, j: (0, j)),
            pl.BlockSpec((d, tf), lambda i, j: (0, nf + j)),
            pl.BlockSpec((tf, d), lambda i, j: (j, 0)),
            pl.BlockSpec((bm, d), lambda i, j: (i, 0)),
        ],
        out_specs=pl.BlockSpec((bm, d), lambda i, j: (i, 0)),
        out_shape=jax.ShapeDtypeStruct((m, d), F32),
        compiler_params=_params("parallel", "arbitrary"),
        name="dense_swiglu",
    )(h, w_gu, w_gu, w_down, x)


ROUTER_BM = 512
MOE_BM = 1024
MOE_TF = 256
GATHER_ROWS = 256
COMBINE_TM = 256
ROUTE_ID, ROUTE_W, ROUTE_RANK = 0, 2, 4


def _router_kernel(x_ref, nw_ref, wr_ref, route_ref, hn_ref, run_ref):
    @pl.when(pl.program_id(0) == 0)
    def _reset_counts():
        run_ref[...] = jnp.zeros_like(run_ref)

    h = _rms_normalize(x_ref[...]) * nw_ref[...]
    bm, d = h.shape
    logits = _dot_f32ish(h, wr_ref[...])
    lane = lax.broadcasted_iota(jnp.int32, logits.shape, 1)
    logits = jnp.where(lane < N_EXPERTS, logits, -jnp.inf)
    m1 = jnp.max(logits, axis=-1, keepdims=True)
    i1 = jnp.min(jnp.where(logits == m1, lane, V7X_LANES), axis=-1, keepdims=True)
    rest = jnp.where(lane == i1, -jnp.inf, logits)
    m2 = jnp.max(rest, axis=-1, keepdims=True)
    i2 = jnp.min(jnp.where(rest == m2, lane, V7X_LANES), axis=-1, keepdims=True)
    e2 = jnp.exp(m2 - m1)
    w1 = 1.0 / (1.0 + e2)
    w2 = e2 / (1.0 + e2)
    oh1 = jnp.where(lane == i1, 1.0, 0.0)
    oh2 = jnp.where(lane == i2, 1.0, 0.0)
    below = jnp.where(lax.broadcasted_iota(jnp.int32, (bm, bm), 0) > lax.broadcasted_iota(jnp.int32, (bm, bm), 1),
                      1.0, 0.0).astype(BF16)
    before = _dot(below, jnp.concatenate([oh1, oh2], axis=1).astype(BF16))
    tot1 = jnp.sum(oh1, axis=0, keepdims=True)
    tot2 = jnp.sum(oh2, axis=0, keepdims=True)
    run = run_ref[...]
    rank1 = jnp.sum(oh1 * (run + before[:, :V7X_LANES]), axis=-1, keepdims=True)
    rank2 = jnp.sum(oh2 * (run + tot1 + before[:, V7X_LANES:]), axis=-1, keepdims=True)
    run_ref[...] = run + tot1 + tot2
    out = jnp.where(lane == ROUTE_ID, i1.astype(F32), 0.0)
    out = jnp.where(lane == ROUTE_ID + 1, i2.astype(F32), out)
    out = jnp.where(lane == ROUTE_W, w1, out)
    out = jnp.where(lane == ROUTE_W + 1, w2, out)
    out = jnp.where(lane == ROUTE_RANK, rank1, out)
    out = jnp.where(lane == ROUTE_RANK + 1, rank2, out)
    route_ref[...] = out
    hn_ref[...] = h


def moe_router(x, norm_w, w_router):
    t, d = x.shape
    bm = ROUTER_BM
    wr = jnp.zeros((d, V7X_LANES), F32).at[:, :N_EXPERTS].set(w_router)
    return pl.pallas_call(
        _router_kernel,
        grid=(t // bm,),
        in_specs=[pl.BlockSpec((bm, d), lambda i: (i, 0)), pl.BlockSpec((1, d), lambda i: (0, 0)),
                  pl.BlockSpec((d, V7X_LANES), lambda i: (0, 0))],
        out_specs=[pl.BlockSpec((bm, V7X_LANES), lambda i: (i, 0)), pl.BlockSpec((bm, d), lambda i: (i, 0))],
        out_shape=[jax.ShapeDtypeStruct((t, V7X_LANES), F32), jax.ShapeDtypeStruct((t, d), F32)],
        scratch_shapes=[pltpu.VMEM((1, V7X_LANES), F32)],
        compiler_params=_params("arbitrary"),
        name="moe_router",
    )(x, norm_w.reshape(1, d), wr)


def _gather_kernel(tok_ref, tv_ref, h_ref, o_ref, buf, sem):
    i = pl.program_id(0)
    base = i * GATHER_ROWS

    def row_copy(r):
        return pltpu.make_async_copy(h_ref.at[pl.ds(tok_ref[base + r], 1), :], buf.at[pl.ds(r, 1), :], sem)

    def issue(r, carry):
        row_copy(r).start()
        return carry

    def drain(r, carry):
        row_copy(r).wait()
        return carry

    @pl.when(tv_ref[i // (MOE_BM // GATHER_ROWS)] == 1)
    def _used_tile():
        lax.fori_loop(0, GATHER_ROWS, issue, 0, unroll=8)
        lax.fori_loop(0, GATHER_ROWS, drain, 0, unroll=8)
        o_ref[...] = buf[...].astype(o_ref.dtype)

    @pl.when(tv_ref[i // (MOE_BM // GATHER_ROWS)] == 0)
    def _unused_tile():
        o_ref[...] = jnp.zeros_like(o_ref)


def moe_gather(row_token, tile_valid, hn):
    assert MOE_BM % GATHER_ROWS == 0
    rows = row_token.shape[0]
    d = hn.shape[1]
    grid_spec = pltpu.PrefetchScalarGridSpec(
        num_scalar_prefetch=2,
        grid=(rows // GATHER_ROWS,),
        in_specs=[pl.BlockSpec(memory_space=pl.ANY)],
        out_specs=pl.BlockSpec((GATHER_ROWS, d), lambda i, tok, tv: (i, 0)),
        scratch_shapes=[pltpu.VMEM((GATHER_ROWS, d), F32), pltpu.SemaphoreType.DMA(())],
    )
    return pl.pallas_call(
        _gather_kernel,
        grid_spec=grid_spec,
        out_shape=jax.ShapeDtypeStruct((rows, d), BF16),
        compiler_params=_params("arbitrary"),
        name="moe_gather",
    )(row_token, tile_valid, hn)


def _moe_ffn_kernel(te_ref, ts_ref, tv_ref, x_ref, wg_ref, wu_ref, wd_ref, o_ref):
    t, j = pl.program_id(0), pl.program_id(1)

    @pl.when(j == 0)
    def _start_tile():
        o_ref[...] = jnp.zeros_like(o_ref)

    @pl.when(tv_ref[t] == 1)
    def _compute():
        x = x_ref[...]
        a = _dot(x, wg_ref[...].astype(BF16))
        u = _dot(x, wu_ref[...].astype(BF16))
        o_ref[...] += _dot((_silu(a) * u).astype(BF16), wd_ref[...].astype(BF16))


def moe_experts(xs, w_gu, w_down, moe_layer, tile_expert, tile_src, tile_valid):
    rows, d = xs.shape
    bm, tf = MOE_BM, MOE_TF
    f = w_down.shape[-2]
    nf = f // tf
    nt = rows // bm

    def jeff(j, tv, t):
        return jnp.where(tv[t] == 1, j, nf - 1)

    grid_spec = pltpu.PrefetchScalarGridSpec(
        num_scalar_prefetch=3,
        grid=(nt, nf),
        in_specs=[
            pl.BlockSpec((bm, d), lambda t, j, te, ts, tv: (ts[t], 0)),
            pl.BlockSpec((None, None, d, tf), lambda t, j, te, ts, tv: (moe_layer, te[t], 0, jeff(j, tv, t))),
            pl.BlockSpec((None, None, d, tf), lambda t, j, te, ts, tv: (moe_layer, te[t], 0, nf + jeff(j, tv, t))),
            pl.BlockSpec((None, None, tf, d), lambda t, j, te, ts, tv: (moe_layer, te[t], jeff(j, tv, t), 0)),
        ],
        out_specs=pl.BlockSpec((bm, d), lambda t, j, te, ts, tv: (t, 0)),
    )
    return pl.pallas_call(
        _moe_ffn_kernel,
        grid_spec=grid_spec,
        out_shape=jax.ShapeDtypeStruct((rows, d), F32),
        compiler_params=_params("arbitrary", "arbitrary"),
        name="moe_experts",
    )(tile_expert, tile_src, tile_valid, xs, w_gu, w_gu, w_down)


def _combine_kernel(pos_ref, x_ref, r_ref, fw_ref, ys_ref, o_ref, buf, sem, *, final_norm):
    tm = COMBINE_TM
    base = pl.program_id(0) * tm

    def row_copy(r, k):
        src = pos_ref[(base + r) * TOP_K + k]
        return pltpu.make_async_copy(ys_ref.at[pl.ds(src, 1), :], buf.at[k, pl.ds(r, 1), :], sem)

    def issue(r, carry):
        for k in range(TOP_K):
            row_copy(r, k).start()
        return carry

    def drain(r, carry):
        for k in range(TOP_K):
            row_copy(r, k).wait()
        return carry

    lax.fori_loop(0, tm, issue, 0, unroll=8)
    lax.fori_loop(0, tm, drain, 0, unroll=8)
    route = r_ref[...]
    y = x_ref[...] + route[:, ROUTE_W:ROUTE_W + 1] * buf[0] + route[:, ROUTE_W + 1:ROUTE_W + 2] * buf[1]
    if final_norm:
        y = _rms_normalize(y) * fw_ref[...]
    o_ref[...] = y


def moe_combine(pos_flat, x, route, ys, final_w, final_norm):
    t, d = x.shape
    tm = COMBINE_TM
    grid_spec = pltpu.PrefetchScalarGridSpec(
        num_scalar_prefetch=1,
        grid=(t // tm,),
        in_specs=[pl.BlockSpec((tm, d), lambda i, pos: (i, 0)),
                  pl.BlockSpec((tm, V7X_LANES), lambda i, pos: (i, 0)),
                  pl.BlockSpec((1, d), lambda i, pos: (0, 0)),
                  pl.BlockSpec(memory_space=pl.ANY)],
        out_specs=pl.BlockSpec((tm, d), lambda i, pos: (i, 0)),
        scratch_shapes=[pltpu.VMEM((TOP_K, tm, d), F32), pltpu.SemaphoreType.DMA(())],
    )
    return pl.pallas_call(
        functools.partial(_combine_kernel, final_norm=final_norm),
        grid_spec=grid_spec,
        out_shape=jax.ShapeDtypeStruct((t, d), F32),
        compiler_params=_params("arbitrary"),
        name="moe_combine",
    )(pos_flat, x, route, final_w.reshape(1, d), ys)


def moe_layer_fn(x, norm_w, w_router, w_gu, w_down, moe_layer, final_w, final_norm):
    t, d = x.shape
    bm = MOE_BM
    route, hn = moe_router(x, norm_w, w_router)
    ids = route[:, ROUTE_ID:ROUTE_ID + TOP_K].astype(jnp.int32)
    ranks = route[:, ROUTE_RANK:ROUTE_RANK + TOP_K].astype(jnp.int32)
    onehot = (ids[..., None] == jnp.arange(N_EXPERTS, dtype=jnp.int32)).astype(jnp.int32)
    counts = jnp.sum(onehot, axis=(0, 1))
    tiles_per = (counts + bm - 1) // bm
    tile_end = jnp.cumsum(tiles_per)
    tile_start = tile_end - tiles_per
    pos = jnp.sum(onehot * tile_start, axis=-1) * bm + ranks
    pos_flat = pos.reshape(-1).astype(jnp.int32)
    nt = (t * TOP_K) // bm + N_EXPERTS
    n_valid = tile_end[-1]
    tile_ids = jnp.arange(nt, dtype=jnp.int32)
    tile_src = jnp.minimum(tile_ids, n_valid - 1).astype(jnp.int32)
    tile_expert = jnp.minimum(jnp.sum(tile_src[:, None] >= tile_end[None, :], axis=1), N_EXPERTS - 1).astype(jnp.int32)
    tile_valid = (tile_ids < n_valid).astype(jnp.int32)
    row_token = jnp.zeros((nt * bm,), jnp.int32).at[pos_flat].set(
        jnp.arange(t * TOP_K, dtype=jnp.int32) // TOP_K, unique_indices=True)
    xs = moe_gather(row_token, tile_valid, hn)
    ys = moe_experts(xs, w_gu, w_down, moe_layer, tile_expert, tile_src, tile_valid)
    return moe_combine(pos_flat, x, route, ys, final_w, final_norm)


def kernel(x, mem, w_in, conv_w, gdn_a_log, gdn_dt_bias, gdn_norm_w, diff_lambda, diff_norm_w, w_branch, w_out,
           mix_norm_w, mem_norm_w, cross_norm_w, cross_w_q, cross_w_kv, cross_w_o, ffn_norm_w, dense_w_gu,
           dense_w_down, router_w, expert_w_gu, expert_w_down, final_norm_w):
    batch, seq, d = x.shape
    mem_len = mem.shape[1]
    t = batch * seq
    x = x.reshape(t, d)
    mem2 = mem.reshape(batch * mem_len, d)
    final_done = False
    for l in range(DEPTH):
        lambda_init = 0.8 - 0.6 * math.exp(-0.3 * l)
        h = rmsnorm(x, mix_norm_w[l], BF16)
        proj = matmul(h, w_in, (l,), 0, OFF_SMALL, bm=1024, bn=1024, out_dtype=F32, name="in_proj_ab")
        small = matmul(h, w_in, (l,), OFF_SMALL // V7X_LANES, V7X_LANES, bm=1024, bn=V7X_LANES, out_dtype=F32,
                       name="in_proj_small")
        tail = matmul_shifted(h, w_in, l, OFF_DIFF, TAIL_W, bm=1024, bn=1024, out_dtype=F32, name="in_proj_tail")
        o_a = retention_mixer(proj, batch, seq)
        o_b = gdn_mixer(proj, small, conv_w, gdn_a_log, gdn_dt_bias, gdn_norm_w, l, batch, seq)
        o_c = diff_attn_mixer(tail, diff_lambda, diff_norm_w, l, lambda_init, batch, seq)
        merged = branch_merge(o_a, o_b, o_c, w_branch, tail, l)
        x = matmul(merged, w_out, (l,), 0, d, bm=1024, bn=1024, out_dtype=F32, res=x, name="out_proj")
        h = rmsnorm(x, cross_norm_w[l], BF16)
        m_n = rmsnorm(mem2, mem_norm_w[l], BF16)
        q = matmul(h, cross_w_q, (l,), 0, MEM_HEADS * MEM_DH, bm=1024, bn=512, out_dtype=BF16, name="cross_q")
        kv = matmul(m_n, cross_w_kv, (l,), 0, 2 * MEM_HEADS * MEM_DH, bm=1024, bn=512, out_dtype=BF16,
                    name="cross_kv")
        att = cross_attention(q, kv, batch, seq, mem_len)
        x = matmul(att, cross_w_o, (l,), 0, d, bm=1024, bn=1024, out_dtype=F32, res=x, name="cross_o")
        if l % 2 == 0:
            h = rmsnorm(x, ffn_norm_w[l], BF16)
            x = dense_ffn(h, dense_w_gu[l // 2].astype(BF16), dense_w_down[l // 2].astype(BF16), x)
        else:
            final_done = l == DEPTH - 1
            x = moe_layer_fn(x, ffn_norm_w[l], router_w[l // 2], expert_w_gu, expert_w_down, l // 2,
                             final_norm_w, final_done)
    if not final_done:
        x = rmsnorm(x, final_norm_w, F32)
    return x.reshape(batch, seq, d)
```

```python
import functools
import math

import jax
import jax.numpy as jnp
from jax import lax
from jax.experimental import pallas as pl
from jax.experimental.pallas import tpu as pltpu

F32 = jnp.float32
BF16 = jnp.bfloat16
U32 = jnp.uint32

D_MODEL = 2048
DEPTH = 2
NORM_EPS = 1e-6
RET_HEADS, RET_DK, RET_CHUNK = 8, 128, 128
GDN_HEADS, GDN_DK, GDN_CONV, GDN_CHUNK = 8, 128, 4, 64
DIFF_HEADS, DIFF_DK, DIFF_DV = 4, 128, 256
MEM_HEADS, MEM_DH = 4, 128
D_FF = 5632
N_EXPERTS, TOP_K, EXPERT_FF = 8, 2, 7168

RET_W = RET_HEADS * RET_DK
GDN_W = GDN_HEADS * GDN_DK
DIFF_W = DIFF_HEADS * DIFF_DV
OFF_GDN = 4 * RET_W
OFF_SMALL = OFF_GDN + 4 * GDN_W
OFF_DIFF = OFF_SMALL + 2 * GDN_HEADS
TAIL_W = 3 * DIFF_W + 3 * D_MODEL

V7X_LANES = 128
V7X_SUBLANES = 8
V7X_MXU_DIM = 256
V7X_VMEM_BYTES = 64 * 1024 * 1024
VMEM_LIMIT = (V7X_VMEM_BYTES * 3) // 4
VMEM_LIMIT_LARGE = (V7X_VMEM_BYTES * 7) // 8

NEG_BIG = -1e30


def _params(*semantics, vmem=VMEM_LIMIT):
    return pltpu.CompilerParams(dimension_semantics=semantics, vmem_limit_bytes=vmem)


def _sigmoid(x):
    return 1.0 / (1.0 + jnp.exp(-x))


def _silu(x):
    return x * _sigmoid(x)


def _dot(a, b):
    return jnp.dot(a, b, preferred_element_type=F32)


def _dot_nt(a, b):
    return lax.dot_general(a, b, (((1,), (1,)), ((), ())), preferred_element_type=F32)


def _dot_tn(a, b):
    return lax.dot_general(a, b, (((0,), (0,)), ((), ())), preferred_element_type=F32)


def _split_bf16(x):
    hi = x.astype(BF16)
    lo = (x - hi.astype(F32)).astype(BF16)
    return hi, lo


def _dot_f32ish(a, b):
    ah, al = _split_bf16(a)
    bh, bl = _split_bf16(b)
    lhs = jnp.concatenate([ah, ah, al], axis=1)
    rhs = jnp.concatenate([bh, bl, bh], axis=0)
    return _dot(lhs, rhs)


def _rms_normalize(x):
    return x * lax.rsqrt(jnp.mean(x * x, axis=-1, keepdims=True) + NORM_EPS)


def _rmsnorm_kernel(x_ref, w_ref, o_ref):
    o_ref[...] = (_rms_normalize(x_ref[...]) * w_ref[...]).astype(o_ref.dtype)


def rmsnorm(x, w, out_dtype, bm=512):
    m, d = x.shape
    bm = min(bm, m)
    return pl.pallas_call(
        _rmsnorm_kernel,
        grid=(m // bm,),
        in_specs=[pl.BlockSpec((bm, d), lambda i: (i, 0)), pl.BlockSpec((1, d), lambda i: (0, 0))],
        out_specs=pl.BlockSpec((bm, d), lambda i: (i, 0)),
        out_shape=jax.ShapeDtypeStruct((m, d), out_dtype),
        compiler_params=_params("parallel"),
        name="rmsnorm",
    )(x, w.reshape(1, d))


def _matmul_kernel(x_ref, w_ref, *rest, has_res):
    if has_res:
        res_ref, o_ref, wbf_ref = rest
    else:
        o_ref, wbf_ref = rest

    @pl.when(pl.program_id(1) == 0)
    def _convert_weight_tile():
        wbf_ref[...] = w_ref[...].astype(BF16)

    acc = _dot(x_ref[...].astype(BF16), wbf_ref[...])
    if has_res:
        acc = acc + res_ref[...]
    o_ref[...] = acc.astype(o_ref.dtype)


def matmul(x, w, w_lead, col_block0, n_out, *, bm, bn, out_dtype, res=None, name="matmul"):
    m, k = x.shape
    bm = min(bm, m)
    bn = min(bn, n_out)
    assert m % bm == 0 and n_out % bn == 0 and w.shape[-2] == k
    n_lead = len(w_lead)
    in_specs = [
        pl.BlockSpec((bm, k), lambda j, i: (i, 0)),
        pl.BlockSpec((None,) * n_lead + (k, bn), lambda j, i: (*w_lead, 0, col_block0 + j)),
    ]
    args = [x, w]
    if res is not None:
        in_specs.append(pl.BlockSpec((bm, bn), lambda j, i: (i, j)))
        args.append(res)
    return pl.pallas_call(
        functools.partial(_matmul_kernel, has_res=res is not None),
        grid=(n_out // bn, m // bm),
        in_specs=in_specs,
        out_specs=pl.BlockSpec((bm, bn), lambda j, i: (i, j)),
        out_shape=jax.ShapeDtypeStruct((m, n_out), out_dtype),
        scratch_shapes=[pltpu.VMEM((k, bn), BF16)],
        compiler_params=_params("arbitrary", "arbitrary"),
        name=name,
    )(*args)


def _matmul_wt_kernel(x_ref, wt_ref, o_ref, wbf_ref):
    @pl.when(pl.program_id(1) == 0)
    def _convert_weight_tile():
        wbf_ref[...] = wt_ref[0].astype(BF16)

    o_ref[...] = _dot_nt(x_ref[...], wbf_ref[...]).astype(o_ref.dtype)


def matmul_wt(x, wt, layer, row0, n_out, *, bm, bn, out_dtype, name):
    m, k = x.shape
    bn = min(bn, n_out)
    assert m % bm == 0 and n_out % bn == 0 and wt.shape[-1] == k
    return pl.pallas_call(
        _matmul_wt_kernel,
        grid=(n_out // bn, m // bm),
        in_specs=[
            pl.BlockSpec((bm, k), lambda j, i: (i, 0)),
            pl.BlockSpec((pl.Element(1), pl.Element(bn), pl.Element(k)),
                         lambda j, i: (layer, pl.multiple_of(row0 + j * bn, V7X_SUBLANES), 0)),
        ],
        out_specs=pl.BlockSpec((bm, bn), lambda j, i: (i, j)),
        out_shape=jax.ShapeDtypeStruct((m, n_out), out_dtype),
        scratch_shapes=[pltpu.VMEM((bn, k), BF16)],
        compiler_params=_params("arbitrary", "arbitrary"),
        name=name,
    )(x, wt)


RET_HEADS_PER_STEP = 2


def _retention_kernel(lg_ref, q_ref, k_ref, v_ref, g_ref, o_ref):
    s = q_ref.shape[0]
    c, dk, nh = RET_CHUNK, RET_DK, RET_HEADS_PER_STEP
    row = lax.broadcasted_iota(jnp.int32, (c, c), 0).astype(F32)
    col = lax.broadcasted_iota(jnp.int32, (c, c), 1).astype(F32)
    diff = row - col
    causal = diff >= 0
    scale = dk ** -0.5
    consts = []
    for hh in range(nh):
        lg = lg_ref[hh]
        decay = jnp.where(causal, jnp.exp(lg * jnp.where(causal, diff, 0.0)), 0.0) * scale
        to_end = jnp.exp(lg * (c - 1.0 - row)) * scale
        from_start = jnp.exp(lg * (row + 1.0))
        chunk_decay = jnp.exp(lg * float(c))
        consts.append((decay, to_end, from_start, chunk_decay))

    def body(i, states):
        r = pl.ds(pl.multiple_of(i * c, c), c)
        new_states = []
        for hh in range(nh):
            decay, to_end, from_start, chunk_decay = consts[hh]
            cs = slice(hh * dk, (hh + 1) * dk)
            kf = k_ref[r, cs]
            qc = q_ref[r, cs].astype(BF16)
            vc = v_ref[r, cs].astype(BF16)
            inner = _dot_nt(qc, kf.astype(BF16)) * decay
            o = _dot(inner.astype(BF16), vc)
            o = o + _dot(qc, states[hh].astype(BF16)) * from_start
            new_states.append(chunk_decay * states[hh] + _dot_tn((kf * to_end).astype(BF16), vc))
            mu = jnp.mean(o, axis=-1, keepdims=True)
            var = jnp.mean(jnp.square(o - mu), axis=-1, keepdims=True)
            on = (o - mu) * lax.rsqrt(var + NORM_EPS)
            o_ref[r, cs] = (on * _silu(g_ref[r, cs])).astype(o_ref.dtype)
        return tuple(new_states)

    init = tuple(jnp.zeros((dk, dk), F32) for _ in range(nh))
    lax.fori_loop(0, s // c, body, init, unroll=2)


def retention_mixer(proj, batch, seq):
    assert RET_DK == RET_CHUNK == V7X_LANES and RET_HEADS % RET_HEADS_PER_STEP == 0
    nh = RET_HEADS_PER_STEP
    steps = RET_HEADS // nh
    log_g = jnp.log(1.0 - 2.0 ** (-5.0 - jnp.arange(RET_HEADS, dtype=F32)))
    lg = jnp.broadcast_to(log_g[:, None, None], (RET_HEADS, 1, V7X_LANES))

    def col(base):
        return pl.BlockSpec((seq, nh * RET_DK), lambda b, hh: (b, base * steps + hh))

    return pl.pallas_call(
        _retention_kernel,
        grid=(batch, steps),
        in_specs=[pl.BlockSpec((nh, 1, V7X_LANES), lambda b, hh: (hh, 0, 0)), col(0), col(1), col(2), col(3)],
        out_specs=pl.BlockSpec((seq, nh * RET_DK), lambda b, hh: (b, hh)),
        out_shape=jax.ShapeDtypeStruct((batch * seq, RET_W), BF16),
        compiler_params=_params("parallel", "parallel"),
        name="retention",
    )(lg, proj, proj, proj, proj)


GDN_GROUP = V7X_MXU_DIM
GDN_HEADS_PER_STEP = 2
GDN_GROUPS_PER_STEP = 2


def _causal_conv_silu(x, w):
    row = lax.broadcasted_iota(jnp.int32, x.shape, 0)
    y = x * w[GDN_CONV - 1:GDN_CONV, :]
    for sh in range(1, GDN_CONV):
        xs = jnp.where(row >= sh, pltpu.roll(x, sh, 0), 0.0)
        y = y + xs * w[GDN_CONV - 1 - sh:GDN_CONV - sh, :]
    return _silu(y)


def _l2_normalize(t):
    return t * lax.rsqrt(jnp.sum(t * t, axis=-1, keepdims=True) + 1e-6)


def _gdn_kernel(xq_ref, xk_ref, xv_ref, z_ref, sm_ref, cwq_ref, cwk_ref, cwv_ref, alog_ref, dtb_ref, nw_ref,
                o_ref, q_s, k_s, v_s, g_s, b_s):
    s = xq_ref.shape[0]
    dk, nh = GDN_DK, GDN_HEADS_PER_STEP
    head0 = pl.program_id(1) * nh
    gsz, c, ng = GDN_GROUP, GDN_CHUNK, GDN_GROUPS_PER_STEP
    n_sub = gsz // c

    qa = _causal_conv_silu(xq_ref[...], cwq_ref[...])
    ka = _causal_conv_silu(xk_ref[...], cwk_ref[...])
    v_s[...] = _causal_conv_silu(xv_ref[...], cwv_ref[...])
    sm = sm_ref[...]
    lane = lax.broadcasted_iota(jnp.int32, sm.shape, 1)
    z_dec = sm + dtb_ref[...]
    softplus = jnp.maximum(z_dec, 0.0) + jnp.log(1.0 + jnp.exp(-jnp.abs(z_dec)))
    g_all = -jnp.exp(alog_ref[...]) * softplus
    beta_all = _sigmoid(sm)
    for hh in range(nh):
        cs = slice(hh * dk, (hh + 1) * dk)
        q_s[:, cs] = _l2_normalize(qa[:, cs]) * (dk ** -0.5)
        k_s[:, cs] = _l2_normalize(ka[:, cs])
        g_col = jnp.sum(jnp.where(lane == head0 + hh + GDN_HEADS, g_all, 0.0), axis=-1, keepdims=True)
        b_col = jnp.sum(jnp.where(lane == head0 + hh, beta_all, 0.0), axis=-1, keepdims=True)
        g_s[:, cs] = jnp.broadcast_to(g_col, (s, dk))
        b_s[:, cs] = jnp.broadcast_to(b_col, (s, dk))

    ri = lax.broadcasted_iota(jnp.int32, (gsz, gsz), 0)
    ci = lax.broadcasted_iota(jnp.int32, (gsz, gsz), 1)
    shift = int(math.log2(c))
    same = jnp.right_shift(ri, shift) == jnp.right_shift(ci, shift)
    incl = same & (ri >= ci)
    strict = same & (ri > ci)
    cum_mat = jnp.concatenate([jnp.where(incl, 1.0, 0.0), jnp.where(same, 1.0, 0.0)], axis=0).astype(BF16)
    nw = nw_ref[...]

    def split3(x):
        x1 = x.astype(BF16)
        r1 = x - x1.astype(F32)
        x2 = r1.astype(BF16)
        x3 = (r1 - x2.astype(F32)).astype(BF16)
        return jnp.concatenate([x1, x2, x3], axis=1)

    def group(gi, states):
        chains = range(ng * nh)
        rows = [pl.ds(pl.multiple_of((gi * ng + ch // nh) * gsz, gsz), gsz) for ch in chains]
        cols = [slice((ch % nh) * dk, (ch % nh + 1) * dk) for ch in chains]
        q = [q_s[rows[ch], cols[ch]] for ch in chains]
        k = [k_s[rows[ch], cols[ch]] for ch in chains]
        v = [v_s[rows[ch], cols[ch]] for ch in chains]
        bb = [b_s[rows[ch], cols[ch]] for ch in chains]
        cums = [_dot(cum_mat, split3(g_s[rows[ch], cols[ch]])) for ch in chains]
        cums = [t[:, :dk] + t[:, dk:2 * dk] + t[:, 2 * dk:] for t in cums]
        gc = [t[:gsz] for t in cums]
        g_last = [t[gsz:] for t in cums]
        kb = [t.astype(BF16) for t in k]
        kk = [_dot_nt(t, t) for t in kb]
        qk = [_dot_nt(q[h].astype(BF16), kb[h]) for h in chains]
        decay_l = []
        for h in chains:
            gc2 = jnp.concatenate([gc[h], gc[h]], axis=1)
            seg = gc2 - gc2.T
            decay_l.append(jnp.where(incl, jnp.exp(jnp.where(incl, seg, 0.0)), 0.0))
        a = [jnp.where(strict, kk[h] * decay_l[h] * jnp.concatenate([bb[h], bb[h]], axis=1), 0.0) for h in chains]
        ab = [t.astype(BF16) for t in a]
        pw = [_dot(t, t) for t in ab]
        rr = [-t for t in a]
        n_round = int(math.log2(c)) - 1
        for it in range(n_round):
            pwb = [t.astype(BF16) for t in pw]
            rr_next = [rr[h] + pw[h] + _dot(rr[h].astype(BF16), pwb[h]) for h in chains]
            if it + 1 < n_round:
                pw = [_dot(t, t) for t in pwb]
            rr = rr_next
        eg = [jnp.exp(t) for t in gc]
        rhs = [jnp.concatenate([k[h] * (bb[h] * eg[h]), v[h] * bb[h]], axis=1) for h in chains]
        wu = [rhs[h] + _dot(rr[h].astype(BF16), rhs[h].astype(BF16)) for h in chains]
        wub = [t.astype(BF16) for t in wu]
        a_qk = [(qk[h] * decay_l[h]).astype(BF16) for h in chains]
        k_tail = [(k[h] * jnp.exp(g_last[h] - gc[h])).astype(BF16) for h in chains]
        e_last = [jnp.exp(t) for t in g_last]
        aw = [_dot(a_qk[h], wub[h]) for h in chains]
        q_eff = [(q[h] * eg[h] - aw[h][:, :dk]).astype(BF16) for h in chains]
        kt = [[_dot_tn(k_tail[h][i * c:(i + 1) * c], wub[h][i * c:(i + 1) * c]) for i in range(n_sub)] for h in chains]
        states = list(states)
        for g in range(ng):
            outs = [[] for _ in range(nh)]
            for i in range(n_sub):
                lo, hi = i * c, (i + 1) * c
                sb = [t.astype(BF16) for t in states]
                new_states = []
                for hh in range(nh):
                    ch = g * nh + hh
                    outs[hh].append(_dot(q_eff[ch][lo:hi], sb[hh]) + aw[ch][lo:hi, dk:])
                    new_states.append(states[hh] * e_last[ch][lo:lo + 1, :]
                                      - _dot(kt[ch][i][:, :dk].astype(BF16), sb[hh]) + kt[ch][i][:, dk:])
                states = new_states
            for hh in range(nh):
                ch = g * nh + hh
                o = jnp.concatenate(outs[hh], axis=0)
                gate = _silu(z_ref[rows[ch], cols[ch]])
                o_ref[rows[ch], cols[ch]] = (_rms_normalize(o) * nw * gate).astype(o_ref.dtype)
        return tuple(states)

    init = tuple(jnp.zeros((dk, dk), F32) for _ in range(nh))
    lax.fori_loop(0, s // (ng * gsz), group, init)


def gdn_mixer(proj, small, conv_w, a_log, dt_bias, norm_w, layer, batch, seq):
    assert GDN_DK == V7X_LANES and GDN_GROUP == 2 * GDN_DK and GDN_GROUP % GDN_CHUNK == 0
    assert seq % (GDN_GROUPS_PER_STEP * GDN_GROUP) == 0
    h, nh = GDN_HEADS, GDN_HEADS_PER_STEP
    steps = h // nh
    wblk = nh * GDN_DK
    base = OFF_GDN // wblk
    pad = V7X_LANES - 2 * h
    alog_row = jnp.concatenate([jnp.zeros((h,), F32), a_log[layer], jnp.zeros((pad,), F32)]).reshape(1, V7X_LANES)
    dtb_row = jnp.concatenate([jnp.zeros((h,), F32), dt_bias[layer], jnp.zeros((pad,), F32)]).reshape(1, V7X_LANES)

    def col(j):
        return pl.BlockSpec((seq, wblk), lambda b, hh: (b, base + j * steps + hh))

    def cw(j):
        return pl.BlockSpec((None, GDN_CONV, wblk), lambda b, hh: (layer, 0, j * steps + hh))

    row_spec = pl.BlockSpec((1, V7X_LANES), lambda b, hh: (0, 0))
    return pl.pallas_call(
        _gdn_kernel,
        grid=(batch, steps),
        in_specs=[col(0), col(1), col(2), col(3),
                  pl.BlockSpec((seq, V7X_LANES), lambda b, hh: (b, 0)),
                  cw(0), cw(1), cw(2), row_spec, row_spec, row_spec],
        out_specs=pl.BlockSpec((seq, wblk), lambda b, hh: (b, hh)),
        out_shape=jax.ShapeDtypeStruct((batch * seq, GDN_W), BF16),
        scratch_shapes=[pltpu.VMEM((seq, wblk), F32)] * 5,
        compiler_params=_params("parallel", "parallel"),
        name="gated_deltanet",
    )(proj, proj, proj, proj, small, conv_w, conv_w, conv_w, alog_row, dtb_row, norm_w[layer].reshape(1, GDN_DK))


DIFF_TQ = 512
DIFF_TK = 512


def _diff_attn_kernel(slope_ref, lam_ref, nw_ref, q_ref, k_ref, v_ref, o_ref, *, out_scale, lambda_init):
    tq, tk, dk = DIFF_TQ, DIFF_TK, DIFF_DK
    qi = pl.program_id(2)
    slope = slope_ref[...][:, :1]
    lv = lam_ref[...]
    lam = (jnp.exp(jnp.sum(lv[0:1] * lv[1:2], axis=-1, keepdims=True))
           - jnp.exp(jnp.sum(lv[2:3] * lv[3:4], axis=-1, keepdims=True)) + lambda_init)
    q = q_ref[...] * (dk ** -0.5)
    q1, q2 = q[:, :dk].astype(BF16), q[:, dk:].astype(BF16)
    key_bias0 = slope * lax.broadcasted_iota(jnp.int32, (1, tk), 1).astype(F32)
    on_or_below_diag = lax.broadcasted_iota(jnp.int32, (tq, tk), 0) >= lax.broadcasted_iota(jnp.int32, (tq, tk), 1)

    def block(j, carry, masked):
        m1, l1, a1, m2, l2, a2 = carry
        r = pl.ds(pl.multiple_of(j * tk, tk), tk)
        kb = k_ref[r, :].astype(BF16)
        vb = v_ref[r, :].astype(BF16)
        key_bias = key_bias0 + slope * (j * tk).astype(F32)

        def update(qh, kh, m, l, acc):
            sc = _dot_nt(qh, kh) + key_bias
            if masked:
                sc = jnp.where(on_or_below_diag, sc, NEG_BIG)
            m_new = jnp.maximum(m, jnp.max(sc, axis=-1, keepdims=True))
            alpha = jnp.exp(m - m_new)
            p = jnp.exp(sc - m_new)
            l = alpha * l + jnp.sum(p, axis=-1, keepdims=True)
            acc = alpha * acc + _dot(p.astype(BF16), vb)
            return m_new, l, acc

        m1, l1, a1 = update(q1, kb[:, :dk], m1, l1, a1)
        m2, l2, a2 = update(q2, kb[:, dk:], m2, l2, a2)
        return m1, l1, a1, m2, l2, a2

    neg = jnp.full((tq, 1), NEG_BIG, F32)
    zero = jnp.zeros((tq, 1), F32)
    zacc = jnp.zeros((tq, DIFF_DV), F32)
    carry = lax.fori_loop(0, qi, lambda j, cr: block(j, cr, False), (neg, zero, zacc, neg, zero, zacc))
    m1, l1, a1, m2, l2, a2 = block(qi, carry, True)
    o = a1 / l1 - lam * (a2 / l2)
    o_ref[...] = (_rms_normalize(o) * nw_ref[...] * out_scale).astype(o_ref.dtype)


def diff_attn_mixer(tail, lam_vecs, norm_w, layer, lambda_init, batch, seq):
    assert DIFF_TQ == DIFF_TK and seq % DIFF_TQ == 0 and 2 * DIFF_DK == DIFF_DV
    h = DIFF_HEADS
    nq = seq // DIFF_TQ
    slopes = 2.0 ** (-8.0 * (jnp.arange(h, dtype=F32) + 1.0) / h)
    slope_rows = jnp.broadcast_to(slopes[:, None, None], (h, 1, V7X_LANES))
    kernel = functools.partial(_diff_attn_kernel, out_scale=1.0 - lambda_init, lambda_init=lambda_init)
    return pl.pallas_call(
        kernel,
        grid=(batch, h, nq),
        in_specs=[
            pl.BlockSpec((None, 1, V7X_LANES), lambda b, hh, i: (hh, 0, 0)),
            pl.BlockSpec((None, 4, DIFF_DK), lambda b, hh, i: (layer, 0, 0)),
            pl.BlockSpec((1, DIFF_DV), lambda b, hh, i: (0, 0)),
            pl.BlockSpec((DIFF_TQ, 2 * DIFF_DK), lambda b, hh, i: (b * nq + i, hh)),
            pl.BlockSpec((seq, 2 * DIFF_DK), lambda b, hh, i: (b, h + hh)),
            pl.BlockSpec((seq, DIFF_DV), lambda b, hh, i: (b, 2 * h + hh)),
        ],
        out_specs=pl.BlockSpec((DIFF_TQ, DIFF_DV), lambda b, hh, i: (b * nq + i, hh)),
        out_shape=jax.ShapeDtypeStruct((batch * seq, DIFF_W), BF16),
        compiler_params=_params("parallel", "parallel", "parallel"),
        name="diff_attention",
    )(slope_rows, lam_vecs, norm_w[layer].reshape(1, DIFF_DV), tail, tail, tail)


def _merge_kernel(oa_ref, ob_ref, oc_ref, wa_ref, wb_ref, wc_ref, ga_ref, gb_ref, gc_ref, o_ref, wbf_ref):
    @pl.when(pl.program_id(1) == 0)
    def _convert_weight_tiles():
        wbf_ref[0] = wa_ref[...].astype(BF16)
        wbf_ref[1] = wb_ref[...].astype(BF16)
        wbf_ref[2] = wc_ref[...].astype(BF16)

    acc = _sigmoid(ga_ref[...]) * _dot(oa_ref[...], wbf_ref[0])
    acc = acc + _sigmoid(gb_ref[...]) * _dot(ob_ref[...], wbf_ref[1])
    acc = acc + _sigmoid(gc_ref[...]) * _dot(oc_ref[...], wbf_ref[2])
    o_ref[...] = acc.astype(o_ref.dtype)


def branch_merge(o_a, o_b, o_c, w_branch, tail, layer, *, bm=1024, bn=512):
    m, kw = o_a.shape
    assert kw == RET_W == GDN_W == DIFF_W
    gate0 = 3 * DIFF_W // bn

    def o_spec():
        return pl.BlockSpec((bm, kw), lambda j, i: (i, 0))

    def w_spec(part):
        return pl.BlockSpec((None, kw, bn), lambda j, i: (layer, part, j))

    def g_spec(part):
        return pl.BlockSpec((bm, bn), lambda j, i: (i, gate0 + part * (D_MODEL // bn) + j))

    return pl.pallas_call(
        _merge_kernel,
        grid=(D_MODEL // bn, m // bm),
        in_specs=[o_spec(), o_spec(), o_spec(), w_spec(0), w_spec(1), w_spec(2), g_spec(0), g_spec(1), g_spec(2)],
        out_specs=pl.BlockSpec((bm, bn), lambda j, i: (i, j)),
        out_shape=jax.ShapeDtypeStruct((m, D_MODEL), BF16),
        scratch_shapes=[pltpu.VMEM((3, kw, bn), BF16)],
        compiler_params=_params("arbitrary", "arbitrary"),
        name="branch_merge",
    )(o_a, o_b, o_c, w_branch, w_branch, w_branch, tail, tail, tail)


CROSS_TQ = 512


def _cross_attn_kernel(q_ref, k_ref, v_ref, o_ref):
    scale = MEM_DH ** -0.5
    outs = []
    for hh in range(MEM_HEADS):
        sl = slice(hh * MEM_DH, (hh + 1) * MEM_DH)
        sc = _dot_nt(q_ref[:, sl], k_ref[:, sl]) * scale
        sc = sc - jnp.max(sc, axis=-1, keepdims=True)
        p = jnp.exp(sc)
        o = _dot(p.astype(BF16), v_ref[:, sl])
        outs.append(o / jnp.sum(p, axis=-1, keepdims=True))
    o_ref[...] = jnp.concatenate(outs, axis=1).astype(o_ref.dtype)


def cross_attention(q, kv, batch, seq, mem_len):
    w = MEM_HEADS * MEM_DH
    nq = seq // CROSS_TQ
    return pl.pallas_call(
        _cross_attn_kernel,
        grid=(batch, nq),
        in_specs=[
            pl.BlockSpec((CROSS_TQ, w), lambda b, i: (b * nq + i, 0)),
            pl.BlockSpec((mem_len, w), lambda b, i: (b, 0)),
            pl.BlockSpec((mem_len, w), lambda b, i: (b, 1)),
        ],
        out_specs=pl.BlockSpec((CROSS_TQ, w), lambda b, i: (b * nq + i, 0)),
        out_shape=jax.ShapeDtypeStruct((batch * seq, w), BF16),
        compiler_params=_params("parallel", "parallel"),
        name="cross_attention",
    )(q, kv, kv)


def _ffn_kernel(h_ref, wg_ref, wu_ref, wd_ref, x_ref, o_ref):
    @pl.when(pl.program_id(1) == 0)
    def _start_from_residual():
        o_ref[...] = x_ref[...]

    h = h_ref[...]
    act = (_silu(_dot(h, wg_ref[...])) * _dot(h, wu_ref[...])).astype(BF16)
    o_ref[...] += _dot(act, wd_ref[...])


def dense_ffn(h, w_gu, w_down, x, *, bm=512, tf=512):
    m, d = h.shape
    f = w_down.shape[0]
    nf = f // tf
    assert f % tf == 0 and m % bm == 0
    return pl.pallas_call(
        _ffn_kernel,
        grid=(m // bm, nf),
        in_specs=[
            pl.BlockSpec((bm, d), lambda i, j: (i, 0)),
            pl.BlockSpec((d, tf), lambda i, j: (0, j)),
            pl.BlockSpec((d, tf), lambda i, j: (0, nf + j)),
            pl.BlockSpec((tf, d), lambda i, j: (j, 0)),
            pl.BlockSpec((bm, d), lambda i, j: (i, 0)),
        ],
        out_specs=pl.BlockSpec((bm, d), lambda i, j: (i, 0)),
        out_shape=jax.ShapeDtypeStruct((m, d), F32),
        compiler_params=_params("parallel", "arbitrary"),
        name="dense_swiglu",
    )(h, w_gu, w_gu, w_down, x)


ROUTER_BM = 512
MOE_BM = 1024
MOE_TF = 256
GATHER_ROWS = 256
COMBINE_TM = 256
ROUTE_ID, ROUTE_W, ROUTE_RANK = 0, 2, 4


def _router_kernel(x_ref, nw_ref, wr_ref, route_ref, hn_ref, run_ref):
    @pl.when(pl.program_id(0) == 0)
    def _reset_counts():
        run_ref[...] = jnp.zeros_like(run_ref)

    h = _rms_normalize(x_ref[...]) * nw_ref[...]
    bm, d = h.shape
    logits = _dot_f32ish(h, wr_ref[...])
    lane = lax.broadcasted_iota(jnp.int32, logits.shape, 1)
    logits = jnp.where(lane < N_EXPERTS, logits, -jnp.inf)
    m1 = jnp.max(logits, axis=-1, keepdims=True)
    i1 = jnp.min(jnp.where(logits == m1, lane, V7X_LANES), axis=-1, keepdims=True)
    rest = jnp.where(lane == i1, -jnp.inf, logits)
    m2 = jnp.max(rest, axis=-1, keepdims=True)
    i2 = jnp.min(jnp.where(rest == m2, lane, V7X_LANES), axis=-1, keepdims=True)
    e2 = jnp.exp(m2 - m1)
    w1 = 1.0 / (1.0 + e2)
    w2 = e2 / (1.0 + e2)
    oh1 = jnp.where(lane == i1, 1.0, 0.0)
    oh2 = jnp.where(lane == i2, 1.0, 0.0)
    below = jnp.where(lax.broadcasted_iota(jnp.int32, (bm, bm), 0) > lax.broadcasted_iota(jnp.int32, (bm, bm), 1),
                      1.0, 0.0).astype(BF16)
    before = _dot(below, jnp.concatenate([oh1, oh2], axis=1).astype(BF16))
    tot1 = jnp.sum(oh1, axis=0, keepdims=True)
    tot2 = jnp.sum(oh2, axis=0, keepdims=True)
    run = run_ref[...]
    rank1 = jnp.sum(oh1 * (run + before[:, :V7X_LANES]), axis=-1, keepdims=True)
    rank2 = jnp.sum(oh2 * (run + tot1 + before[:, V7X_LANES:]), axis=-1, keepdims=True)
    run_ref[...] = run + tot1 + tot2
    out = jnp.where(lane == ROUTE_ID, i1.astype(F32), 0.0)
    out = jnp.where(lane == ROUTE_ID + 1, i2.astype(F32), out)
    out = jnp.where(lane == ROUTE_W, w1, out)
    out = jnp.where(lane == ROUTE_W + 1, w2, out)
    out = jnp.where(lane == ROUTE_RANK, rank1, out)
    out = jnp.where(lane == ROUTE_RANK + 1, rank2, out)
    route_ref[...] = out
    hn_ref[...] = h


def moe_router(x, norm_w, w_router):
    t, d = x.shape
    bm = ROUTER_BM
    wr = jnp.zeros((d, V7X_LANES), F32).at[:, :N_EXPERTS].set(w_router)
    return pl.pallas_call(
        _router_kernel,
        grid=(t // bm,),
        in_specs=[pl.BlockSpec((bm, d), lambda i: (i, 0)), pl.BlockSpec((1, d), lambda i: (0, 0)),
                  pl.BlockSpec((d, V7X_LANES), lambda i: (0, 0))],
        out_specs=[pl.BlockSpec((bm, V7X_LANES), lambda i: (i, 0)), pl.BlockSpec((bm, d), lambda i: (i, 0))],
        out_shape=[jax.ShapeDtypeStruct((t, V7X_LANES), F32), jax.ShapeDtypeStruct((t, d), F32)],
        scratch_shapes=[pltpu.VMEM((1, V7X_LANES), F32)],
        compiler_params=_params("arbitrary"),
        name="moe_router",
    )(x, norm_w.reshape(1, d), wr)


def _gather_kernel(tok_ref, tr_ref, h_ref, o_ref, buf, sem):
    i = pl.program_id(0)
    base = i * GATHER_ROWS
    steps_per_tile = MOE_BM // GATHER_ROWS
    n_used = jnp.clip(tr_ref[i // steps_per_tile] - (i % steps_per_tile) * GATHER_ROWS, 0, GATHER_ROWS)

    def row_copy(r):
        return pltpu.make_async_copy(h_ref.at[pl.ds(tok_ref[base + r], 1), :], buf.at[pl.ds(r, 1), :], sem)

    def issue(r, carry):
        row_copy(r).start()
        return carry

    def drain(r, carry):
        row_copy(r).wait()
        return carry

    @pl.when(n_used == GATHER_ROWS)
    def _full_step():
        lax.fori_loop(0, GATHER_ROWS, issue, 0, unroll=8)
        lax.fori_loop(0, GATHER_ROWS, drain, 0, unroll=8)
        o_ref[...] = buf[...].astype(o_ref.dtype)

    @pl.when((n_used > 0) & (n_used < GATHER_ROWS))
    def _partial_step():
        buf[...] = jnp.zeros_like(buf)
        lax.fori_loop(0, n_used, issue, 0)
        lax.fori_loop(0, n_used, drain, 0)
        o_ref[...] = buf[...].astype(o_ref.dtype)

    @pl.when(n_used == 0)
    def _padding_step():
        o_ref[...] = jnp.zeros_like(o_ref)


def moe_gather(row_token, tile_rows, hn):
    assert MOE_BM % GATHER_ROWS == 0
    rows = row_token.shape[0]
    d = hn.shape[1]
    grid_spec = pltpu.PrefetchScalarGridSpec(
        num_scalar_prefetch=2,
        grid=(rows // GATHER_ROWS,),
        in_specs=[pl.BlockSpec(memory_space=pl.ANY)],
        out_specs=pl.BlockSpec((GATHER_ROWS, d), lambda i, tok, tv: (i, 0)),
        scratch_shapes=[pltpu.VMEM((GATHER_ROWS, d), F32), pltpu.SemaphoreType.DMA(())],
    )
    return pl.pallas_call(
        _gather_kernel,
        grid_spec=grid_spec,
        out_shape=jax.ShapeDtypeStruct((rows, d), BF16),
        compiler_params=_params("arbitrary"),
        name="moe_gather",
    )(row_token, tile_rows, hn)


MOE_ROW_STEPS = (MOE_BM // 4, MOE_BM // 2, MOE_BM)


def _moe_ffn_kernel(te_ref, ts_ref, tr_ref, x_ref, wg_ref, wu_ref, wd_ref, o_ref):
    t, j = pl.program_id(0), pl.program_id(1)
    n_rows = tr_ref[t]

    @pl.when(j == 0)
    def _start_tile():
        o_ref[...] = jnp.zeros_like(o_ref)

    def evaluate(m):
        x = x_ref[:m, :]
        a = _dot(x, wg_ref[...].astype(BF16))
        u = _dot(x, wu_ref[...].astype(BF16))
        o_ref[:m, :] += _dot((_silu(a) * u).astype(BF16), wd_ref[...].astype(BF16))

    lo = 0
    for m in MOE_ROW_STEPS:
        pl.when((n_rows > lo) & (n_rows <= m))(functools.partial(evaluate, m))
        lo = m


def moe_experts(xs, w_gu, w_down, moe_layer, tile_expert, tile_src, tile_rows):
    rows, d = xs.shape
    bm, tf = MOE_BM, MOE_TF
    f = w_down.shape[-2]
    nf = f // tf
    nt = rows // bm

    def jeff(j, tr, t):
        return jnp.where(tr[t] > 0, j, nf - 1)

    grid_spec = pltpu.PrefetchScalarGridSpec(
        num_scalar_prefetch=3,
        grid=(nt, nf),
        in_specs=[
            pl.BlockSpec((bm, d), lambda t, j, te, ts, tv: (ts[t], 0)),
            pl.BlockSpec((None, None, d, tf), lambda t, j, te, ts, tv: (moe_layer, te[t], 0, jeff(j, tv, t))),
            pl.BlockSpec((None, None, d, tf), lambda t, j, te, ts, tv: (moe_layer, te[t], 0, nf + jeff(j, tv, t))),
            pl.BlockSpec((None, None, tf, d), lambda t, j, te, ts, tv: (moe_layer, te[t], jeff(j, tv, t), 0)),
        ],
        out_specs=pl.BlockSpec((bm, d), lambda t, j, te, ts, tv: (t, 0)),
    )
    return pl.pallas_call(
        _moe_ffn_kernel,
        grid_spec=grid_spec,
        out_shape=jax.ShapeDtypeStruct((rows, d), F32),
        compiler_params=_params("arbitrary", "arbitrary"),
        name="moe_experts",
    )(tile_expert, tile_src, tile_rows, xs, w_gu, w_gu, w_down)


def _combine_kernel(pos_ref, x_ref, r_ref, fw_ref, ys_ref, o_ref, buf, sem, *, final_norm):
    tm = COMBINE_TM
    base = pl.program_id(0) * tm

    def row_copy(r, k):
        src = pos_ref[(base + r) * TOP_K + k]
        return pltpu.make_async_copy(ys_ref.at[pl.ds(src, 1), :], buf.at[k, pl.ds(r, 1), :], sem)

    def issue(r, carry):
        for k in range(TOP_K):
            row_copy(r, k).start()
        return carry

    def drain(r, carry):
        for k in range(TOP_K):
            row_copy(r, k).wait()
        return carry

    lax.fori_loop(0, tm, issue, 0, unroll=8)
    lax.fori_loop(0, tm, drain, 0, unroll=8)
    route = r_ref[...]
    y = x_ref[...] + route[:, ROUTE_W:ROUTE_W + 1] * buf[0] + route[:, ROUTE_W + 1:ROUTE_W + 2] * buf[1]
    if final_norm:
        y = _rms_normalize(y) * fw_ref[...]
    o_ref[...] = y


def moe_combine(pos_flat, x, route, ys, final_w, final_norm):
    t, d = x.shape
    tm = COMBINE_TM
    grid_spec = pltpu.PrefetchScalarGridSpec(
        num_scalar_prefetch=1,
        grid=(t // tm,),
        in_specs=[pl.BlockSpec((tm, d), lambda i, pos: (i, 0)),
                  pl.BlockSpec((tm, V7X_LANES), lambda i, pos: (i, 0)),
                  pl.BlockSpec((1, d), lambda i, pos: (0, 0)),
                  pl.BlockSpec(memory_space=pl.ANY)],
        out_specs=pl.BlockSpec((tm, d), lambda i, pos: (i, 0)),
        scratch_shapes=[pltpu.VMEM((TOP_K, tm, d), F32), pltpu.SemaphoreType.DMA(())],
    )
    return pl.pallas_call(
        functools.partial(_combine_kernel, final_norm=final_norm),
        grid_spec=grid_spec,
        out_shape=jax.ShapeDtypeStruct((t, d), F32),
        compiler_params=_params("arbitrary"),
        name="moe_combine",
    )(pos_flat, x, route, final_w.reshape(1, d), ys)


def moe_layer_fn(x, norm_w, w_router, w_gu, w_down, moe_layer, final_w, final_norm):
    t, d = x.shape
    bm = MOE_BM
    route, hn = moe_router(x, norm_w, w_router)
    ids = route[:, ROUTE_ID:ROUTE_ID + TOP_K].astype(jnp.int32)
    ranks = route[:, ROUTE_RANK:ROUTE_RANK + TOP_K].astype(jnp.int32)
    onehot = (ids[..., None] == jnp.arange(N_EXPERTS, dtype=jnp.int32)).astype(jnp.int32)
    counts = jnp.sum(onehot, axis=(0, 1))
    tiles_per = (counts + bm - 1) // bm
    tile_end = jnp.cumsum(tiles_per)
    tile_start = tile_end - tiles_per
    pos = jnp.sum(onehot * tile_start, axis=-1) * bm + ranks
    pos_flat = pos.reshape(-1).astype(jnp.int32)
    nt = (t * TOP_K) // bm + N_EXPERTS
    n_valid = tile_end[-1]
    tile_ids = jnp.arange(nt, dtype=jnp.int32)
    tile_src = jnp.minimum(tile_ids, n_valid - 1).astype(jnp.int32)
    tile_expert = jnp.minimum(jnp.sum(tile_src[:, None] >= tile_end[None, :], axis=1), N_EXPERTS - 1).astype(jnp.int32)
    tile_rows = jnp.clip(counts[tile_expert] - (tile_ids - tile_start[tile_expert]) * bm, 0, bm)
    tile_rows = jnp.where(tile_ids < n_valid, tile_rows, 0).astype(jnp.int32)
    row_token = jnp.zeros((nt * bm,), jnp.int32).at[pos_flat].set(
        jnp.arange(t * TOP_K, dtype=jnp.int32) // TOP_K, unique_indices=True)
    xs = moe_gather(row_token, tile_rows, hn)
    ys = moe_experts(xs, w_gu, w_down, moe_layer, tile_expert, tile_src, tile_rows)
    return moe_combine(pos_flat, x, route, ys, final_w, final_norm)


def kernel(x, mem, w_in, conv_w, gdn_a_log, gdn_dt_bias, gdn_norm_w, diff_lambda, diff_norm_w, w_branch, w_out,
           mix_norm_w, mem_norm_w, cross_norm_w, cross_w_q, cross_w_kv, cross_w_o, ffn_norm_w, dense_w_gu,
           dense_w_down, router_w, expert_w_gu, expert_w_down, final_norm_w):
    batch, seq, d = x.shape
    mem_len = mem.shape[1]
    t = batch * seq
    x = x.reshape(t, d)
    mem2 = mem.reshape(batch * mem_len, d)
    final_done = False
    w_in_t = jnp.swapaxes(w_in, 1, 2)
    for l in range(DEPTH):
        lambda_init = 0.8 - 0.6 * math.exp(-0.3 * l)
        h = rmsnorm(x, mix_norm_w[l], BF16)
        proj = matmul_wt(h, w_in_t, l, 0, OFF_SMALL, bm=1024, bn=1024, out_dtype=F32, name="in_proj_ab")
        small = matmul_wt(h, w_in_t, l, OFF_SMALL, V7X_LANES, bm=1024, bn=V7X_LANES, out_dtype=F32,
                          name="in_proj_small")
        tail = matmul_wt(h, w_in_t, l, OFF_DIFF, TAIL_W, bm=1024, bn=1024, out_dtype=F32, name="in_proj_tail")
        o_a = retention_mixer(proj, batch, seq)
        o_b = gdn_mixer(proj, small, conv_w, gdn_a_log, gdn_dt_bias, gdn_norm_w, l, batch, seq)
        o_c = diff_attn_mixer(tail, diff_lambda, diff_norm_w, l, lambda_init, batch, seq)
        merged = branch_merge(o_a, o_b, o_c, w_branch, tail, l)
        x = matmul(merged, w_out, (l,), 0, d, bm=1024, bn=1024, out_dtype=F32, res=x, name="out_proj")
        h = rmsnorm(x, cross_norm_w[l], BF16)
        m_n = rmsnorm(mem2, mem_norm_w[l], BF16)
        q = matmul(h, cross_w_q, (l,), 0, MEM_HEADS * MEM_DH, bm=1024, bn=512, out_dtype=BF16, name="cross_q")
        kv = matmul(m_n, cross_w_kv, (l,), 0, 2 * MEM_HEADS * MEM_DH, bm=1024, bn=512, out_dtype=BF16,
                    name="cross_kv")
        att = cross_attention(q, kv, batch, seq, mem_len)
        x = matmul(att, cross_w_o, (l,), 0, d, bm=1024, bn=1024, out_dtype=F32, res=x, name="cross_o")
        if l % 2 == 0:
            h = rmsnorm(x, ffn_norm_w[l], BF16)
            x = dense_ffn(h, dense_w_gu[l // 2].astype(BF16), dense_w_down[l // 2].astype(BF16), x)
        else:
            final_done = l == DEPTH - 1
            x = moe_layer_fn(x, ffn_norm_w[l], router_w[l // 2], expert_w_gu, expert_w_down, l // 2,
                             final_norm_w, final_done)
    if not final_done:
        x = rmsnorm(x, final_norm_w, F32)
    return x.reshape(batch, seq, d)
```

```python
import functools
import math

import jax
import jax.numpy as jnp
from jax import lax
from jax.experimental import pallas as pl
from jax.experimental.pallas import tpu as pltpu

F32 = jnp.float32
BF16 = jnp.bfloat16
U32 = jnp.uint32

D_MODEL = 2048
DEPTH = 2
NORM_EPS = 1e-6
RET_HEADS, RET_DK, RET_CHUNK = 8, 128, 128
GDN_HEADS, GDN_DK, GDN_CONV, GDN_CHUNK = 8, 128, 4, 64
DIFF_HEADS, DIFF_DK, DIFF_DV = 4, 128, 256
MEM_HEADS, MEM_DH = 4, 128
D_FF = 5632
N_EXPERTS, TOP_K, EXPERT_FF = 8, 2, 7168

RET_W = RET_HEADS * RET_DK
GDN_W = GDN_HEADS * GDN_DK
DIFF_W = DIFF_HEADS * DIFF_DV
OFF_GDN = 4 * RET_W
OFF_SMALL = OFF_GDN + 4 * GDN_W
OFF_DIFF = OFF_SMALL + 2 * GDN_HEADS
TAIL_W = 3 * DIFF_W + 3 * D_MODEL

V7X_LANES = 128
V7X_SUBLANES = 8
V7X_MXU_DIM = 256
V7X_VMEM_BYTES = 64 * 1024 * 1024
VMEM_LIMIT = (V7X_VMEM_BYTES * 3) // 4
VMEM_LIMIT_LARGE = (V7X_VMEM_BYTES * 7) // 8

NEG_BIG = -1e30


def _params(*semantics, vmem=VMEM_LIMIT):
    return pltpu.CompilerParams(dimension_semantics=semantics, vmem_limit_bytes=vmem)


def _sigmoid(x):
    return 1.0 / (1.0 + jnp.exp(-x))


def _silu(x):
    return x * _sigmoid(x)


def _dot(a, b):
    return jnp.dot(a, b, preferred_element_type=F32)


def _dot_nt(a, b):
    return lax.dot_general(a, b, (((1,), (1,)), ((), ())), preferred_element_type=F32)


def _dot_tn(a, b):
    return lax.dot_general(a, b, (((0,), (0,)), ((), ())), preferred_element_type=F32)


def _split_bf16(x):
    hi = x.astype(BF16)
    lo = (x - hi.astype(F32)).astype(BF16)
    return hi, lo


def _dot_f32ish(a, b):
    ah, al = _split_bf16(a)
    bh, bl = _split_bf16(b)
    lhs = jnp.concatenate([ah, ah, al], axis=1)
    rhs = jnp.concatenate([bh, bl, bh], axis=0)
    return _dot(lhs, rhs)


def _rms_normalize(x):
    return x * lax.rsqrt(jnp.mean(x * x, axis=-1, keepdims=True) + NORM_EPS)


def _rmsnorm_kernel(x_ref, w_ref, o_ref):
    o_ref[...] = (_rms_normalize(x_ref[...]) * w_ref[...]).astype(o_ref.dtype)


def rmsnorm(x, w, out_dtype, bm=512):
    m, d = x.shape
    bm = min(bm, m)
    return pl.pallas_call(
        _rmsnorm_kernel,
        grid=(m // bm,),
        in_specs=[pl.BlockSpec((bm, d), lambda i: (i, 0)), pl.BlockSpec((1, d), lambda i: (0, 0))],
        out_specs=pl.BlockSpec((bm, d), lambda i: (i, 0)),
        out_shape=jax.ShapeDtypeStruct((m, d), out_dtype),
        compiler_params=_params("parallel"),
        name="rmsnorm",
    )(x, w.reshape(1, d))


def _matmul_kernel(x_ref, w_ref, *rest, has_res):
    if has_res:
        res_ref, o_ref, wbf_ref = rest
    else:
        o_ref, wbf_ref = rest

    @pl.when(pl.program_id(1) == 0)
    def _convert_weight_tile():
        wbf_ref[...] = w_ref[...].astype(BF16)

    acc = _dot(x_ref[...].astype(BF16), wbf_ref[...])
    if has_res:
        acc = acc + res_ref[...]
    o_ref[...] = acc.astype(o_ref.dtype)


def matmul(x, w, w_lead, col_block0, n_out, *, bm, bn, out_dtype, res=None, name="matmul"):
    m, k = x.shape
    bm = min(bm, m)
    bn = min(bn, n_out)
    assert m % bm == 0 and n_out % bn == 0 and w.shape[-2] == k
    n_lead = len(w_lead)
    in_specs = [
        pl.BlockSpec((bm, k), lambda j, i: (i, 0)),
        pl.BlockSpec((None,) * n_lead + (k, bn), lambda j, i: (*w_lead, 0, col_block0 + j)),
    ]
    args = [x, w]
    if res is not None:
        in_specs.append(pl.BlockSpec((bm, bn), lambda j, i: (i, j)))
        args.append(res)
    return pl.pallas_call(
        functools.partial(_matmul_kernel, has_res=res is not None),
        grid=(n_out // bn, m // bm),
        in_specs=in_specs,
        out_specs=pl.BlockSpec((bm, bn), lambda j, i: (i, j)),
        out_shape=jax.ShapeDtypeStruct((m, n_out), out_dtype),
        scratch_shapes=[pltpu.VMEM((k, bn), BF16)],
        compiler_params=_params("arbitrary", "arbitrary"),
        name=name,
    )(*args)


def _matmul_wt_kernel(x_ref, wt_ref, o_ref, wbf_ref):
    @pl.when(pl.program_id(1) == 0)
    def _convert_weight_tile():
        wbf_ref[...] = wt_ref[0].astype(BF16)

    o_ref[...] = _dot_nt(x_ref[...], wbf_ref[...]).astype(o_ref.dtype)


def matmul_wt(x, wt, layer, row0, n_out, *, bm, bn, out_dtype, name):
    m, k = x.shape
    bn = min(bn, n_out)
    assert m % bm == 0 and n_out % bn == 0 and wt.shape[-1] == k
    return pl.pallas_call(
        _matmul_wt_kernel,
        grid=(n_out // bn, m // bm),
        in_specs=[
            pl.BlockSpec((bm, k), lambda j, i: (i, 0)),
            pl.BlockSpec((pl.Element(1), pl.Element(bn), pl.Element(k)),
                         lambda j, i: (layer, pl.multiple_of(row0 + j * bn, V7X_SUBLANES), 0)),
        ],
        out_specs=pl.BlockSpec((bm, bn), lambda j, i: (i, j)),
        out_shape=jax.ShapeDtypeStruct((m, n_out), out_dtype),
        scratch_shapes=[pltpu.VMEM((bn, k), BF16)],
        compiler_params=_params("arbitrary", "arbitrary"),
        name=name,
    )(x, wt)


RET_HEADS_PER_STEP = 4


def _retention_kernel(lg_ref, q_ref, k_ref, v_ref, g_ref, o_ref):
    s = q_ref.shape[0]
    c, dk, nh = RET_CHUNK, RET_DK, RET_HEADS_PER_STEP
    row = lax.broadcasted_iota(jnp.int32, (c, c), 0).astype(F32)
    col = lax.broadcasted_iota(jnp.int32, (c, c), 1).astype(F32)
    diff = row - col
    causal = diff >= 0
    scale = dk ** -0.5
    consts = []
    for hh in range(nh):
        lg = lg_ref[hh]
        decay = jnp.where(causal, jnp.exp(lg * jnp.where(causal, diff, 0.0)), 0.0) * scale
        to_end = jnp.exp(lg * (c - 1.0 - row)) * scale
        from_start = jnp.exp(lg * (row + 1.0))
        chunk_decay = jnp.exp(lg * float(c))
        consts.append((decay, to_end, from_start, chunk_decay))

    def body(i, states):
        r = pl.ds(pl.multiple_of(i * c, c), c)
        new_states = []
        for hh in range(nh):
            decay, to_end, from_start, chunk_decay = consts[hh]
            cs = slice(hh * dk, (hh + 1) * dk)
            kf = k_ref[r, cs]
            qc = q_ref[r, cs].astype(BF16)
            vc = v_ref[r, cs].astype(BF16)
            inner = _dot_nt(qc, kf.astype(BF16)) * decay
            o = _dot(inner.astype(BF16), vc)
            o = o + _dot(qc, states[hh].astype(BF16)) * from_start
            new_states.append(chunk_decay * states[hh] + _dot_tn((kf * to_end).astype(BF16), vc))
            mu = jnp.mean(o, axis=-1, keepdims=True)
            var = jnp.mean(jnp.square(o - mu), axis=-1, keepdims=True)
            on = (o - mu) * lax.rsqrt(var + NORM_EPS)
            o_ref[r, cs] = (on * _silu(g_ref[r, cs])).astype(o_ref.dtype)
        return tuple(new_states)

    init = tuple(jnp.zeros((dk, dk), F32) for _ in range(nh))
    lax.fori_loop(0, s // c, body, init, unroll=2)


def retention_mixer(proj, batch, seq):
    assert RET_DK == RET_CHUNK == V7X_LANES and RET_HEADS % RET_HEADS_PER_STEP == 0
    nh = RET_HEADS_PER_STEP
    steps = RET_HEADS // nh
    log_g = jnp.log(1.0 - 2.0 ** (-5.0 - jnp.arange(RET_HEADS, dtype=F32)))
    lg = jnp.broadcast_to(log_g[:, None, None], (RET_HEADS, 1, V7X_LANES))

    def col(base):
        return pl.BlockSpec((seq, nh * RET_DK), lambda b, hh: (b, base * steps + hh))

    return pl.pallas_call(
        _retention_kernel,
        grid=(batch, steps),
        in_specs=[pl.BlockSpec((nh, 1, V7X_LANES), lambda b, hh: (hh, 0, 0)), col(0), col(1), col(2), col(3)],
        out_specs=pl.BlockSpec((seq, nh * RET_DK), lambda b, hh: (b, hh)),
        out_shape=jax.ShapeDtypeStruct((batch * seq, RET_W), BF16),
        compiler_params=_params("parallel", "parallel"),
        name="retention",
    )(lg, proj, proj, proj, proj)


GDN_GROUP = V7X_MXU_DIM
GDN_HEADS_PER_STEP = 2
GDN_GROUPS_PER_STEP = 2


def _causal_conv_silu(x, w):
    row = lax.broadcasted_iota(jnp.int32, x.shape, 0)
    y = x * w[GDN_CONV - 1:GDN_CONV, :]
    for sh in range(1, GDN_CONV):
        xs = jnp.where(row >= sh, pltpu.roll(x, sh, 0), 0.0)
        y = y + xs * w[GDN_CONV - 1 - sh:GDN_CONV - sh, :]
    return _silu(y)


def _l2_normalize(t):
    return t * lax.rsqrt(jnp.sum(t * t, axis=-1, keepdims=True) + 1e-6)


def _gdn_kernel(xq_ref, xk_ref, xv_ref, z_ref, sm_ref, cwq_ref, cwk_ref, cwv_ref, alog_ref, dtb_ref, nw_ref,
                o_ref, q_s, k_s, v_s, g_s, b_s):
    s = xq_ref.shape[0]
    dk, nh = GDN_DK, GDN_HEADS_PER_STEP
    head0 = pl.program_id(1) * nh
    gsz, c, ng = GDN_GROUP, GDN_CHUNK, GDN_GROUPS_PER_STEP
    n_sub = gsz // c

    qa = _causal_conv_silu(xq_ref[...], cwq_ref[...])
    ka = _causal_conv_silu(xk_ref[...], cwk_ref[...])
    v_s[...] = _causal_conv_silu(xv_ref[...], cwv_ref[...])
    sm = sm_ref[...]
    lane = lax.broadcasted_iota(jnp.int32, sm.shape, 1)
    z_dec = sm + dtb_ref[...]
    softplus = jnp.maximum(z_dec, 0.0) + jnp.log(1.0 + jnp.exp(-jnp.abs(z_dec)))
    g_all = -jnp.exp(alog_ref[...]) * softplus
    beta_all = _sigmoid(sm)
    for hh in range(nh):
        cs = slice(hh * dk, (hh + 1) * dk)
        q_s[:, cs] = _l2_normalize(qa[:, cs]) * (dk ** -0.5)
        k_s[:, cs] = _l2_normalize(ka[:, cs])
        g_col = jnp.sum(jnp.where(lane == head0 + hh + GDN_HEADS, g_all, 0.0), axis=-1, keepdims=True)
        b_col = jnp.sum(jnp.where(lane == head0 + hh, beta_all, 0.0), axis=-1, keepdims=True)
        g_s[:, cs] = jnp.broadcast_to(g_col, (s, dk))
        b_s[:, cs] = jnp.broadcast_to(b_col, (s, dk))

    ri = lax.broadcasted_iota(jnp.int32, (gsz, gsz), 0)
    ci = lax.broadcasted_iota(jnp.int32, (gsz, gsz), 1)
    shift = int(math.log2(c))
    same = jnp.right_shift(ri, shift) == jnp.right_shift(ci, shift)
    incl = same & (ri >= ci)
    strict = same & (ri > ci)
    cum_mat = jnp.concatenate([jnp.where(incl, 1.0, 0.0), jnp.where(same, 1.0, 0.0)], axis=0).astype(BF16)
    nw = nw_ref[...]

    def split3(x):
        x1 = x.astype(BF16)
        r1 = x - x1.astype(F32)
        x2 = r1.astype(BF16)
        x3 = (r1 - x2.astype(F32)).astype(BF16)
        return jnp.concatenate([x1, x2, x3], axis=1)

    def group(gi, states):
        chains = range(ng * nh)
        rows = [pl.ds(pl.multiple_of((gi * ng + ch // nh) * gsz, gsz), gsz) for ch in chains]
        cols = [slice((ch % nh) * dk, (ch % nh + 1) * dk) for ch in chains]
        q = [q_s[rows[ch], cols[ch]] for ch in chains]
        k = [k_s[rows[ch], cols[ch]] for ch in chains]
        v = [v_s[rows[ch], cols[ch]] for ch in chains]
        bb = [b_s[rows[ch], cols[ch]] for ch in chains]
        cums = [_dot(cum_mat, split3(g_s[rows[ch], cols[ch]])) for ch in chains]
        cums = [t[:, :dk] + t[:, dk:2 * dk] + t[:, 2 * dk:] for t in cums]
        gc = [t[:gsz] for t in cums]
        g_last = [t[gsz:] for t in cums]
        kb = [t.astype(BF16) for t in k]
        kk = [_dot_nt(t, t) for t in kb]
        qk = [_dot_nt(q[h].astype(BF16), kb[h]) for h in chains]
        decay_l = []
        for h in chains:
            gc2 = jnp.concatenate([gc[h], gc[h]], axis=1)
            seg = gc2 - gc2.T
            decay_l.append(jnp.where(incl, jnp.exp(jnp.where(incl, seg, 0.0)), 0.0))
        a = [jnp.where(strict, kk[h] * decay_l[h] * jnp.concatenate([bb[h], bb[h]], axis=1), 0.0) for h in chains]
        ab = [t.astype(BF16) for t in a]
        pw = [_dot(t, t) for t in ab]
        rr = [-t for t in a]
        n_round = int(math.log2(c)) - 1
        for it in range(n_round):
            pwb = [t.astype(BF16) for t in pw]
            rr_next = [rr[h] + pw[h] + _dot(rr[h].astype(BF16), pwb[h]) for h in chains]
            if it + 1 < n_round:
                pw = [_dot(t, t) for t in pwb]
            rr = rr_next
        eg = [jnp.exp(t) for t in gc]
        rhs = [jnp.concatenate([k[h] * (bb[h] * eg[h]), v[h] * bb[h]], axis=1) for h in chains]
        wu = [rhs[h] + _dot(rr[h].astype(BF16), rhs[h].astype(BF16)) for h in chains]
        wub = [t.astype(BF16) for t in wu]
        a_qk = [(qk[h] * decay_l[h]).astype(BF16) for h in chains]
        k_tail = [(k[h] * jnp.exp(g_last[h] - gc[h])).astype(BF16) for h in chains]
        e_last = [jnp.exp(t) for t in g_last]
        aw = [_dot(a_qk[h], wub[h]) for h in chains]
        q_eff = [(q[h] * eg[h] - aw[h][:, :dk]).astype(BF16) for h in chains]
        kt = [[_dot_tn(k_tail[h][i * c:(i + 1) * c], wub[h][i * c:(i + 1) * c]) for i in range(n_sub)] for h in chains]
        states = list(states)
        for g in range(ng):
            outs = [[] for _ in range(nh)]
            for i in range(n_sub):
                lo, hi = i * c, (i + 1) * c
                sb = [t.astype(BF16) for t in states]
                new_states = []
                for hh in range(nh):
                    ch = g * nh + hh
                    outs[hh].append(_dot(q_eff[ch][lo:hi], sb[hh]) + aw[ch][lo:hi, dk:])
                    new_states.append(states[hh] * e_last[ch][lo:lo + 1, :]
                                      - _dot(kt[ch][i][:, :dk].astype(BF16), sb[hh]) + kt[ch][i][:, dk:])
                states = new_states
            for hh in range(nh):
                ch = g * nh + hh
                o = jnp.concatenate(outs[hh], axis=0)
                gate = _silu(z_ref[rows[ch], cols[ch]])
                o_ref[rows[ch], cols[ch]] = (_rms_normalize(o) * nw * gate).astype(o_ref.dtype)
        return tuple(states)

    init = tuple(jnp.zeros((dk, dk), F32) for _ in range(nh))
    lax.fori_loop(0, s // (ng * gsz), group, init)


def gdn_mixer(proj, small, conv_w, a_log, dt_bias, norm_w, layer, batch, seq):
    assert GDN_DK == V7X_LANES and GDN_GROUP == 2 * GDN_DK and GDN_GROUP % GDN_CHUNK == 0
    assert seq % (GDN_GROUPS_PER_STEP * GDN_GROUP) == 0
    h, nh = GDN_HEADS, GDN_HEADS_PER_STEP
    steps = h // nh
    wblk = nh * GDN_DK
    base = OFF_GDN // wblk
    pad = V7X_LANES - 2 * h
    alog_row = jnp.concatenate([jnp.zeros((h,), F32), a_log[layer], jnp.zeros((pad,), F32)]).reshape(1, V7X_LANES)
    dtb_row = jnp.concatenate([jnp.zeros((h,), F32), dt_bias[layer], jnp.zeros((pad,), F32)]).reshape(1, V7X_LANES)

    def col(j):
        return pl.BlockSpec((seq, wblk), lambda b, hh: (b, base + j * steps + hh))

    def cw(j):
        return pl.BlockSpec((None, GDN_CONV, wblk), lambda b, hh: (layer, 0, j * steps + hh))

    row_spec = pl.BlockSpec((1, V7X_LANES), lambda b, hh: (0, 0))
    return pl.pallas_call(
        _gdn_kernel,
        grid=(batch, steps),
        in_specs=[col(0), col(1), col(2), col(3),
                  pl.BlockSpec((seq, V7X_LANES), lambda b, hh: (b, 0)),
                  cw(0), cw(1), cw(2), row_spec, row_spec, row_spec],
        out_specs=pl.BlockSpec((seq, wblk), lambda b, hh: (b, hh)),
        out_shape=jax.ShapeDtypeStruct((batch * seq, GDN_W), BF16),
        scratch_shapes=[pltpu.VMEM((seq, wblk), F32)] * 5,
        compiler_params=_params("parallel", "parallel"),
        name="gated_deltanet",
    )(proj, proj, proj, proj, small, conv_w, conv_w, conv_w, alog_row, dtb_row, norm_w[layer].reshape(1, GDN_DK))


DIFF_TQ = 512
DIFF_TK = 512


def _diff_attn_kernel(slope_ref, lam_ref, nw_ref, q_ref, k_ref, v_ref, o_ref, *, out_scale, lambda_init):
    tq, tk, dk = DIFF_TQ, DIFF_TK, DIFF_DK
    qi = pl.program_id(2)
    slope = slope_ref[...][:, :1]
    lv = lam_ref[...]
    lam = (jnp.exp(jnp.sum(lv[0:1] * lv[1:2], axis=-1, keepdims=True))
           - jnp.exp(jnp.sum(lv[2:3] * lv[3:4], axis=-1, keepdims=True)) + lambda_init)
    q = q_ref[...] * (dk ** -0.5)
    q1, q2 = q[:, :dk].astype(BF16), q[:, dk:].astype(BF16)
    key_bias0 = slope * lax.broadcasted_iota(jnp.int32, (1, tk), 1).astype(F32)
    on_or_below_diag = lax.broadcasted_iota(jnp.int32, (tq, tk), 0) >= lax.broadcasted_iota(jnp.int32, (tq, tk), 1)

    def block(j, carry, masked):
        m1, l1, a1, m2, l2, a2 = carry
        r = pl.ds(pl.multiple_of(j * tk, tk), tk)
        kb = k_ref[r, :].astype(BF16)
        vb = v_ref[r, :].astype(BF16)
        key_bias = key_bias0 + slope * (j * tk).astype(F32)

        def update(qh, kh, m, l, acc):
            sc = _dot_nt(qh, kh) + key_bias
            if masked:
                sc = jnp.where(on_or_below_diag, sc, NEG_BIG)
            m_new = jnp.maximum(m, jnp.max(sc, axis=-1, keepdims=True))
            alpha = jnp.exp(m - m_new)
            p = jnp.exp(sc - m_new)
            l = alpha * l + jnp.sum(p, axis=-1, keepdims=True)
            acc = alpha * acc + _dot(p.astype(BF16), vb)
            return m_new, l, acc

        m1, l1, a1 = update(q1, kb[:, :dk], m1, l1, a1)
        m2, l2, a2 = update(q2, kb[:, dk:], m2, l2, a2)
        return m1, l1, a1, m2, l2, a2

    neg = jnp.full((tq, 1), NEG_BIG, F32)
    zero = jnp.zeros((tq, 1), F32)
    zacc = jnp.zeros((tq, DIFF_DV), F32)
    carry = lax.fori_loop(0, qi, lambda j, cr: block(j, cr, False), (neg, zero, zacc, neg, zero, zacc))
    m1, l1, a1, m2, l2, a2 = block(qi, carry, True)
    o = a1 / l1 - lam * (a2 / l2)
    o_ref[...] = (_rms_normalize(o) * nw_ref[...] * out_scale).astype(o_ref.dtype)


def diff_attn_mixer(tail, lam_vecs, norm_w, layer, lambda_init, batch, seq):
    assert DIFF_TQ == DIFF_TK and seq % DIFF_TQ == 0 and 2 * DIFF_DK == DIFF_DV
    h = DIFF_HEADS
    nq = seq // DIFF_TQ
    slopes = 2.0 ** (-8.0 * (jnp.arange(h, dtype=F32) + 1.0) / h)
    slope_rows = jnp.broadcast_to(slopes[:, None, None], (h, 1, V7X_LANES))
    kernel = functools.partial(_diff_attn_kernel, out_scale=1.0 - lambda_init, lambda_init=lambda_init)
    return pl.pallas_call(
        kernel,
        grid=(batch, h, nq),
        in_specs=[
            pl.BlockSpec((None, 1, V7X_LANES), lambda b, hh, i: (hh, 0, 0)),
            pl.BlockSpec((None, 4, DIFF_DK), lambda b, hh, i: (layer, 0, 0)),
            pl.BlockSpec((1, DIFF_DV), lambda b, hh, i: (0, 0)),
            pl.BlockSpec((DIFF_TQ, 2 * DIFF_DK), lambda b, hh, i: (b * nq + i, hh)),
            pl.BlockSpec((seq, 2 * DIFF_DK), lambda b, hh, i: (b, h + hh)),
            pl.BlockSpec((seq, DIFF_DV), lambda b, hh, i: (b, 2 * h + hh)),
        ],
        out_specs=pl.BlockSpec((DIFF_TQ, DIFF_DV), lambda b, hh, i: (b * nq + i, hh)),
        out_shape=jax.ShapeDtypeStruct((batch * seq, DIFF_W), BF16),
        compiler_params=_params("parallel", "parallel", "parallel"),
        name="diff_attention",
    )(slope_rows, lam_vecs, norm_w[layer].reshape(1, DIFF_DV), tail, tail, tail)


def _merge_kernel(oa_ref, ob_ref, oc_ref, wa_ref, wb_ref, wc_ref, ga_ref, gb_ref, gc_ref, o_ref, wbf_ref):
    @pl.when(pl.program_id(1) == 0)
    def _convert_weight_tiles():
        wbf_ref[0] = wa_ref[...].astype(BF16)
        wbf_ref[1] = wb_ref[...].astype(BF16)
        wbf_ref[2] = wc_ref[...].astype(BF16)

    acc = _sigmoid(ga_ref[...]) * _dot(oa_ref[...], wbf_ref[0])
    acc = acc + _sigmoid(gb_ref[...]) * _dot(ob_ref[...], wbf_ref[1])
    acc = acc + _sigmoid(gc_ref[...]) * _dot(oc_ref[...], wbf_ref[2])
    o_ref[...] = acc.astype(o_ref.dtype)


def branch_merge(o_a, o_b, o_c, w_branch, tail, layer, *, bm=1024, bn=512):
    m, kw = o_a.shape
    assert kw == RET_W == GDN_W == DIFF_W
    gate0 = 3 * DIFF_W // bn

    def o_spec():
        return pl.BlockSpec((bm, kw), lambda j, i: (i, 0))

    def w_spec(part):
        return pl.BlockSpec((None, kw, bn), lambda j, i: (layer, part, j))

    def g_spec(part):
        return pl.BlockSpec((bm, bn), lambda j, i: (i, gate0 + part * (D_MODEL // bn) + j))

    return pl.pallas_call(
        _merge_kernel,
        grid=(D_MODEL // bn, m // bm),
        in_specs=[o_spec(), o_spec(), o_spec(), w_spec(0), w_spec(1), w_spec(2), g_spec(0), g_spec(1), g_spec(2)],
        out_specs=pl.BlockSpec((bm, bn), lambda j, i: (i, j)),
        out_shape=jax.ShapeDtypeStruct((m, D_MODEL), BF16),
        scratch_shapes=[pltpu.VMEM((3, kw, bn), BF16)],
        compiler_params=_params("arbitrary", "arbitrary"),
        name="branch_merge",
    )(o_a, o_b, o_c, w_branch, w_branch, w_branch, tail, tail, tail)


CROSS_TQ = 512


def _cross_attn_kernel(q_ref, k_ref, v_ref, o_ref):
    scale = MEM_DH ** -0.5
    outs = []
    for hh in range(MEM_HEADS):
        sl = slice(hh * MEM_DH, (hh + 1) * MEM_DH)
        sc = _dot_nt(q_ref[:, sl], k_ref[:, sl]) * scale
        sc = sc - jnp.max(sc, axis=-1, keepdims=True)
        p = jnp.exp(sc)
        o = _dot(p.astype(BF16), v_ref[:, sl])
        outs.append(o / jnp.sum(p, axis=-1, keepdims=True))
    o_ref[...] = jnp.concatenate(outs, axis=1).astype(o_ref.dtype)


def cross_attention(q, kv, batch, seq, mem_len):
    w = MEM_HEADS * MEM_DH
    nq = seq // CROSS_TQ
    return pl.pallas_call(
        _cross_attn_kernel,
        grid=(batch, nq),
        in_specs=[
            pl.BlockSpec((CROSS_TQ, w), lambda b, i: (b * nq + i, 0)),
            pl.BlockSpec((mem_len, w), lambda b, i: (b, 0)),
            pl.BlockSpec((mem_len, w), lambda b, i: (b, 1)),
        ],
        out_specs=pl.BlockSpec((CROSS_TQ, w), lambda b, i: (b * nq + i, 0)),
        out_shape=jax.ShapeDtypeStruct((batch * seq, w), BF16),
        compiler_params=_params("parallel", "parallel"),
        name="cross_attention",
    )(q, kv, kv)


def _ffn_kernel(h_ref, wg_ref, wu_ref, wd_ref, x_ref, o_ref):
    @pl.when(pl.program_id(1) == 0)
    def _start_from_residual():
        o_ref[...] = x_ref[...]

    h = h_ref[...]
    act = (_silu(_dot(h, wg_ref[...])) * _dot(h, wu_ref[...])).astype(BF16)
    o_ref[...] += _dot(act, wd_ref[...])


def dense_ffn(h, w_gu, w_down, x, *, bm=512, tf=512):
    m, d = h.shape
    f = w_down.shape[0]
    nf = f // tf
    assert f % tf == 0 and m % bm == 0
    return pl.pallas_call(
        _ffn_kernel,
        grid=(m // bm, nf),
        in_specs=[
            pl.BlockSpec((bm, d), lambda i, j: (i, 0)),
            pl.BlockSpec((d, tf), lambda i, j: (0, j)),
            pl.BlockSpec((d, tf), lambda i, j: (0, nf + j)),
            pl.BlockSpec((tf, d), lambda i, j: (j, 0)),
            pl.BlockSpec((bm, d), lambda i, j: (i, 0)),
        ],
        out_specs=pl.BlockSpec((bm, d), lambda i, j: (i, 0)),
        out_shape=jax.ShapeDtypeStruct((m, d), F32),
        compiler_params=_params("parallel", "arbitrary"),
        name="dense_swiglu",
    )(h, w_gu, w_gu, w_down, x)


ROUTER_BM = 512
MOE_BM = 1024
MOE_TF = 256
GATHER_ROWS = 512
COMBINE_TM = 512
ROUTE_ID, ROUTE_W, ROUTE_RANK = 0, 2, 4


def _router_kernel(x_ref, nw_ref, wr_ref, route_ref, hn_ref, run_ref):
    @pl.when(pl.program_id(0) == 0)
    def _reset_counts():
        run_ref[...] = jnp.zeros_like(run_ref)

    h = _rms_normalize(x_ref[...]) * nw_ref[...]
    bm, d = h.shape
    logits = _dot_f32ish(h, wr_ref[...])
    lane = lax.broadcasted_iota(jnp.int32, logits.shape, 1)
    logits = jnp.where(lane < N_EXPERTS, logits, -jnp.inf)
    m1 = jnp.max(logits, axis=-1, keepdims=True)
    i1 = jnp.min(jnp.where(logits == m1, lane, V7X_LANES), axis=-1, keepdims=True)
    rest = jnp.where(lane == i1, -jnp.inf, logits)
    m2 = jnp.max(rest, axis=-1, keepdims=True)
    i2 = jnp.min(jnp.where(rest == m2, lane, V7X_LANES), axis=-1, keepdims=True)
    e2 = jnp.exp(m2 - m1)
    w1 = 1.0 / (1.0 + e2)
    w2 = e2 / (1.0 + e2)
    oh1 = jnp.where(lane == i1, 1.0, 0.0)
    oh2 = jnp.where(lane == i2, 1.0, 0.0)
    below = jnp.where(lax.broadcasted_iota(jnp.int32, (bm, bm), 0) > lax.broadcasted_iota(jnp.int32, (bm, bm), 1),
                      1.0, 0.0).astype(BF16)
    before = _dot(below, jnp.concatenate([oh1, oh2], axis=1).astype(BF16))
    tot1 = jnp.sum(oh1, axis=0, keepdims=True)
    tot2 = jnp.sum(oh2, axis=0, keepdims=True)
    run = run_ref[...]
    rank1 = jnp.sum(oh1 * (run + before[:, :V7X_LANES]), axis=-1, keepdims=True)
    rank2 = jnp.sum(oh2 * (run + tot1 + before[:, V7X_LANES:]), axis=-1, keepdims=True)
    run_ref[...] = run + tot1 + tot2
    out = jnp.where(lane == ROUTE_ID, i1.astype(F32), 0.0)
    out = jnp.where(lane == ROUTE_ID + 1, i2.astype(F32), out)
    out = jnp.where(lane == ROUTE_W, w1, out)
    out = jnp.where(lane == ROUTE_W + 1, w2, out)
    out = jnp.where(lane == ROUTE_RANK, rank1, out)
    out = jnp.where(lane == ROUTE_RANK + 1, rank2, out)
    route_ref[...] = out
    hn_ref[...] = h


def moe_router(x, norm_w, w_router):
    t, d = x.shape
    bm = ROUTER_BM
    wr = jnp.zeros((d, V7X_LANES), F32).at[:, :N_EXPERTS].set(w_router)
    return pl.pallas_call(
        _router_kernel,
        grid=(t // bm,),
        in_specs=[pl.BlockSpec((bm, d), lambda i: (i, 0)), pl.BlockSpec((1, d), lambda i: (0, 0)),
                  pl.BlockSpec((d, V7X_LANES), lambda i: (0, 0))],
        out_specs=[pl.BlockSpec((bm, V7X_LANES), lambda i: (i, 0)), pl.BlockSpec((bm, d), lambda i: (i, 0))],
        out_shape=[jax.ShapeDtypeStruct((t, V7X_LANES), F32), jax.ShapeDtypeStruct((t, d), F32)],
        scratch_shapes=[pltpu.VMEM((1, V7X_LANES), F32)],
        compiler_params=_params("arbitrary"),
        name="moe_router",
    )(x, norm_w.reshape(1, d), wr)


def _gather_kernel(tok_ref, tr_ref, h_ref, o_ref, buf, sem):
    i = pl.program_id(0)
    base = i * GATHER_ROWS
    steps_per_tile = MOE_BM // GATHER_ROWS
    n_used = jnp.clip(tr_ref[i // steps_per_tile] - (i % steps_per_tile) * GATHER_ROWS, 0, GATHER_ROWS)

    def row_copy(r):
        return pltpu.make_async_copy(h_ref.at[pl.ds(tok_ref[base + r], 1), :], buf.at[pl.ds(r, 1), :], sem)

    def issue(r, carry):
        row_copy(r).start()
        return carry

    def drain(r, carry):
        row_copy(r).wait()
        return carry

    @pl.when(n_used == GATHER_ROWS)
    def _full_step():
        lax.fori_loop(0, GATHER_ROWS, issue, 0, unroll=8)
        lax.fori_loop(0, GATHER_ROWS, drain, 0, unroll=8)
        o_ref[...] = buf[...].astype(o_ref.dtype)

    @pl.when((n_used > 0) & (n_used < GATHER_ROWS))
    def _partial_step():
        buf[...] = jnp.zeros_like(buf)
        lax.fori_loop(0, n_used, issue, 0)
        lax.fori_loop(0, n_used, drain, 0)
        o_ref[...] = buf[...].astype(o_ref.dtype)

    @pl.when(n_used == 0)
    def _padding_step():
        o_ref[...] = jnp.zeros_like(o_ref)


def moe_gather(row_token, tile_rows, hn):
    assert MOE_BM % GATHER_ROWS == 0
    rows = row_token.shape[0]
    d = hn.shape[1]
    grid_spec = pltpu.PrefetchScalarGridSpec(
        num_scalar_prefetch=2,
        grid=(rows // GATHER_ROWS,),
        in_specs=[pl.BlockSpec(memory_space=pl.ANY)],
        out_specs=pl.BlockSpec((GATHER_ROWS, d), lambda i, tok, tv: (i, 0)),
        scratch_shapes=[pltpu.VMEM((GATHER_ROWS, d), F32), pltpu.SemaphoreType.DMA(())],
    )
    return pl.pallas_call(
        _gather_kernel,
        grid_spec=grid_spec,
        out_shape=jax.ShapeDtypeStruct((rows, d), BF16),
        compiler_params=_params("arbitrary"),
        name="moe_gather",
    )(row_token, tile_rows, hn)


MOE_ROW_STEPS = (MOE_BM // 8, MOE_BM // 4, MOE_BM // 2, MOE_BM)


def _moe_ffn_kernel(te_ref, ts_ref, tr_ref, x_ref, wg_ref, wu_ref, wd_ref, o_ref):
    t, j = pl.program_id(0), pl.program_id(1)
    n_rows = tr_ref[t]

    @pl.when(j == 0)
    def _start_tile():
        o_ref[...] = jnp.zeros_like(o_ref)

    def evaluate(m):
        x = x_ref[:m, :]
        a = _dot(x, wg_ref[...].astype(BF16))
        u = _dot(x, wu_ref[...].astype(BF16))
        o_ref[:m, :] += _dot((_silu(a) * u).astype(BF16), wd_ref[...].astype(BF16))

    lo = 0
    for m in MOE_ROW_STEPS:
        pl.when((n_rows > lo) & (n_rows <= m))(functools.partial(evaluate, m))
        lo = m


def moe_experts(xs, w_gu, w_down, moe_layer, tile_expert, tile_src, tile_rows):
    rows, d = xs.shape
    bm, tf = MOE_BM, MOE_TF
    f = w_down.shape[-2]
    nf = f // tf
    nt = rows // bm

    def jeff(j, tr, t):
        return jnp.where(tr[t] > 0, j, nf - 1)

    grid_spec = pltpu.PrefetchScalarGridSpec(
        num_scalar_prefetch=3,
        grid=(nt, nf),
        in_specs=[
            pl.BlockSpec((bm, d), lambda t, j, te, ts, tv: (ts[t], 0)),
            pl.BlockSpec((None, None, d, tf), lambda t, j, te, ts, tv: (moe_layer, te[t], 0, jeff(j, tv, t))),
            pl.BlockSpec((None, None, d, tf), lambda t, j, te, ts, tv: (moe_layer, te[t], 0, nf + jeff(j, tv, t))),
            pl.BlockSpec((None, None, tf, d), lambda t, j, te, ts, tv: (moe_layer, te[t], jeff(j, tv, t), 0)),
        ],
        out_specs=pl.BlockSpec((bm, d), lambda t, j, te, ts, tv: (t, 0)),
    )
    return pl.pallas_call(
        _moe_ffn_kernel,
        grid_spec=grid_spec,
        out_shape=jax.ShapeDtypeStruct((rows, d), F32),
        compiler_params=_params("arbitrary", "arbitrary"),
        name="moe_experts",
    )(tile_expert, tile_src, tile_rows, xs, w_gu, w_gu, w_down)


def _combine_kernel(pos_ref, x_ref, r_ref, fw_ref, ys_ref, o_ref, buf, sem, *, final_norm):
    tm = COMBINE_TM
    base = pl.program_id(0) * tm

    def row_copy(r, k):
        src = pos_ref[(base + r) * TOP_K + k]
        return pltpu.make_async_copy(ys_ref.at[pl.ds(src, 1), :], buf.at[k, pl.ds(r, 1), :], sem)

    def issue(r, carry):
        for k in range(TOP_K):
            row_copy(r, k).start()
        return carry

    def drain(r, carry):
        for k in range(TOP_K):
            row_copy(r, k).wait()
        return carry

    lax.fori_loop(0, tm, issue, 0, unroll=8)
    lax.fori_loop(0, tm, drain, 0, unroll=8)
    route = r_ref[...]
    y = x_ref[...] + route[:, ROUTE_W:ROUTE_W + 1] * buf[0] + route[:, ROUTE_W + 1:ROUTE_W + 2] * buf[1]
    if final_norm:
        y = _rms_normalize(y) * fw_ref[...]
    o_ref[...] = y


def moe_combine(pos_flat, x, route, ys, final_w, final_norm):
    t, d = x.shape
    tm = COMBINE_TM
    grid_spec = pltpu.PrefetchScalarGridSpec(
        num_scalar_prefetch=1,
        grid=(t // tm,),
        in_specs=[pl.BlockSpec((tm, d), lambda i, pos: (i, 0)),
                  pl.BlockSpec((tm, V7X_LANES), lambda i, pos: (i, 0)),
                  pl.BlockSpec((1, d), lambda i, pos: (0, 0)),
                  pl.BlockSpec(memory_space=pl.ANY)],
        out_specs=pl.BlockSpec((tm, d), lambda i, pos: (i, 0)),
        scratch_shapes=[pltpu.VMEM((TOP_K, tm, d), F32), pltpu.SemaphoreType.DMA(())],
    )
    return pl.pallas_call(
        functools.partial(_combine_kernel, final_norm=final_norm),
        grid_spec=grid_spec,
        out_shape=jax.ShapeDtypeStruct((t, d), F32),
        compiler_params=_params("arbitrary"),
        name="moe_combine",
    )(pos_flat, x, route, final_w.reshape(1, d), ys)


def moe_layer_fn(x, norm_w, w_router, w_gu, w_down, moe_layer, final_w, final_norm):
    t, d = x.shape
    bm = MOE_BM
    route, hn = moe_router(x, norm_w, w_router)
    ids = route[:, ROUTE_ID:ROUTE_ID + TOP_K].astype(jnp.int32)
    ranks = route[:, ROUTE_RANK:ROUTE_RANK + TOP_K].astype(jnp.int32)
    onehot = (ids[..., None] == jnp.arange(N_EXPERTS, dtype=jnp.int32)).astype(jnp.int32)
    counts = jnp.sum(onehot, axis=(0, 1))
    tiles_per = (counts + bm - 1) // bm
    tile_end = jnp.cumsum(tiles_per)
    tile_start = tile_end - tiles_per
    pos = jnp.sum(onehot * tile_start, axis=-1) * bm + ranks
    pos_flat = pos.reshape(-1).astype(jnp.int32)
    nt = (t * TOP_K) // bm + N_EXPERTS
    n_valid = tile_end[-1]
    tile_ids = jnp.arange(nt, dtype=jnp.int32)
    tile_src = jnp.minimum(tile_ids, n_valid - 1).astype(jnp.int32)
    tile_expert = jnp.minimum(jnp.sum(tile_src[:, None] >= tile_end[None, :], axis=1), N_EXPERTS - 1).astype(jnp.int32)
    tile_rows = jnp.clip(counts[tile_expert] - (tile_ids - tile_start[tile_expert]) * bm, 0, bm)
    tile_rows = jnp.where(tile_ids < n_valid, tile_rows, 0).astype(jnp.int32)
    row_token = jnp.zeros((nt * bm,), jnp.int32).at[pos_flat].set(
        jnp.arange(t * TOP_K, dtype=jnp.int32) // TOP_K, unique_indices=True)
    xs = moe_gather(row_token, tile_rows, hn)
    ys = moe_experts(xs, w_gu, w_down, moe_layer, tile_expert, tile_src, tile_rows)
    return moe_combine(pos_flat, x, route, ys, final_w, final_norm)


def kernel(x, mem, w_in, conv_w, gdn_a_log, gdn_dt_bias, gdn_norm_w, diff_lambda, diff_norm_w, w_branch, w_out,
           mix_norm_w, mem_norm_w, cross_norm_w, cross_w_q, cross_w_kv, cross_w_o, ffn_norm_w, dense_w_gu,
           dense_w_down, router_w, expert_w_gu, expert_w_down, final_norm_w):
    batch, seq, d = x.shape
    mem_len = mem.shape[1]
    t = batch * seq
    x = x.reshape(t, d)
    mem2 = mem.reshape(batch * mem_len, d)
    final_done = False
    w_in_t = jnp.swapaxes(w_in, 1, 2)
    for l in range(DEPTH):
        lambda_init = 0.8 - 0.6 * math.exp(-0.3 * l)
        h = rmsnorm(x, mix_norm_w[l], BF16)
        proj = matmul_wt(h, w_in_t, l, 0, OFF_SMALL, bm=1024, bn=1024, out_dtype=F32, name="in_proj_ab")
        small = matmul_wt(h, w_in_t, l, OFF_SMALL, V7X_LANES, bm=1024, bn=V7X_LANES, out_dtype=F32,
                          name="in_proj_small")
        tail = matmul_wt(h, w_in_t, l, OFF_DIFF, TAIL_W, bm=1024, bn=1024, out_dtype=F32, name="in_proj_tail")
        o_a = retention_mixer(proj, batch, seq)
        o_b = gdn_mixer(proj, small, conv_w, gdn_a_log, gdn_dt_bias, gdn_norm_w, l, batch, seq)
        o_c = diff_attn_mixer(tail, diff_lambda, diff_norm_w, l, lambda_init, batch, seq)
        merged = branch_merge(o_a, o_b, o_c, w_branch, tail, l)
        x = matmul(merged, w_out, (l,), 0, d, bm=1024, bn=1024, out_dtype=F32, res=x, name="out_proj")
        h = rmsnorm(x, cross_norm_w[l], BF16)
        m_n = rmsnorm(mem2, mem_norm_w[l], BF16)
        q = matmul(h, cross_w_q, (l,), 0, MEM_HEADS * MEM_DH, bm=1024, bn=512, out_dtype=BF16, name="cross_q")
        kv = matmul(m_n, cross_w_kv, (l,), 0, 2 * MEM_HEADS * MEM_DH, bm=1024, bn=512, out_dtype=BF16,
                    name="cross_kv")
        att = cross_attention(q, kv, batch, seq, mem_len)
        x = matmul(att, cross_w_o, (l,), 0, d, bm=1024, bn=1024, out_dtype=F32, res=x, name="cross_o")
        if l % 2 == 0:
            h = rmsnorm(x, ffn_norm_w[l], BF16)
            x = dense_ffn(h, dense_w_gu[l // 2].astype(BF16), dense_w_down[l // 2].astype(BF16), x)
        else:
            final_done = l == DEPTH - 1
            x = moe_layer_fn(x, ffn_norm_w[l], router_w[l // 2], expert_w_gu, expert_w_down, l // 2,
                             final_norm_w, final_done)
    if not final_done:
        x = rmsnorm(x, final_norm_w, F32)
    return x.reshape(batch, seq, d)
```
